```python
import math
import jax, jax.numpy as jnp
from jax import lax
import numpy as np

D_MODEL = 4096
BATCH = 4
SEQ = 4096
DEPTH = 4

HEAD_DIM = 128
N_HEADS_DIL = 16
N_HEADS_SB = 16
D_DIL = N_HEADS_DIL * HEAD_DIM
D_SB = N_HEADS_SB * HEAD_DIM
D_MIX = D_DIL + D_SB
DILATED_CONFIGS = ((128, 1), (512, 4), (2048, 16))
BLOCK = 128
N_BUCKETS = 32
MAX_DISTANCE = 2048
N_GROUPS = 4
EXPERTS_PER_GROUP = 8
N_EXPERTS = N_GROUPS * EXPERTS_PER_GROUP
D_EXPERT = 256
TOP_K_IN_GROUP = 2
RMS_EPS = 1e-6
NEG_INF = -1e30

kernel_name = "hybrid_dilated_stickbreaking_hmoe"


def rmsnorm(x, g):
    xf = x.astype(jnp.float32)
    var = jnp.mean(xf * xf, axis=-1, keepdims=True)
    return (xf * lax.rsqrt(var + RMS_EPS)).astype(x.dtype) * g


def t5_bucket(dist):
    max_exact = N_BUCKETS // 2
    d_f = jnp.maximum(dist, 1).astype(jnp.float32)
    large = max_exact + (jnp.log(d_f / max_exact) / math.log(MAX_DISTANCE / max_exact)
                         * (N_BUCKETS - max_exact)).astype(jnp.int32)
    large = jnp.minimum(large, N_BUCKETS - 1)
    return jnp.where(dist < max_exact, dist, large)


def dilated_branch(q, k, v, window, dilation, rel_bias):
    B, H, S, Dh = q.shape
    L = S // dilation
    nb = -(-L // BLOCK)
    Lp = nb * BLOCK
    w_units = window // dilation

    def sub(t):
        t = jnp.moveaxis(t.reshape(B, H, L, dilation, Dh), 3, 2)
        t = jnp.pad(t, ((0, 0), (0, 0), (0, 0), (0, Lp - L), (0, 0)))
        return t.reshape(B, H, dilation, nb, BLOCK, Dh)

    def with_prev(t):
        prev = jnp.pad(t[:, :, :, :-1], ((0, 0), (0, 0), (0, 0), (1, 0), (0, 0), (0, 0)))
        return jnp.concatenate([prev, t], axis=4)

    qs = sub(q)
    kc = with_prev(sub(k))
    vc = with_prev(sub(v))
    logits = jnp.einsum('bhgnqd,bhgnkd->bhgnqk', qs, kc,
                        preferred_element_type=jnp.float32) * (HEAD_DIM ** -0.5)
    qi = jnp.arange(BLOCK)[:, None]
    kj = jnp.arange(2 * BLOCK)[None, :]
    dist = BLOCK + qi - kj
    band = (dist >= 0) & (dist <= w_units)
    valid = band[None] & ((jnp.arange(nb)[:, None, None] > 0) | (kj >= BLOCK)[None])
    bucket = t5_bucket(jnp.maximum(dist, 0) * dilation)
    bias = jnp.transpose(jnp.take(rel_bias, bucket, axis=0), (2, 0, 1)).astype(jnp.float32)
    logits = logits + bias[None, :, None, None]
    logits = jnp.where(valid[None, None, None], logits, NEG_INF)
    lse = jax.nn.logsumexp(logits, axis=-1)
    p = jnp.exp(logits - lse[..., None])
    out = jnp.einsum('bhgnqk,bhgnkd->bhgnqd', p.astype(v.dtype), vc)

    def unsub(t):
        rest = t.shape[5:]
        t = t.reshape((B, H, dilation, Lp) + rest)[:, :, :, :L]
        t = jnp.moveaxis(t, 2, 3)
        return t.reshape((B, H, S) + rest)

    return unsub(out), unsub(lse)


def dilated_mixer(q, k, v, rel_bias):
    outs, lses = [], []
    for window, dilation in DILATED_CONFIGS:
        o, l = dilated_branch(q, k, v, window, dilation, rel_bias)
        outs.append(o.astype(jnp.float32))
        lses.append(l)
    w = jax.nn.softmax(jnp.stack(lses, axis=0), axis=0)
    out = jnp.sum(w[..., None] * jnp.stack(outs, axis=0), axis=0)
    return out.astype(q.dtype)


def stick_breaking_mixer(q, k, v):
    B, H, S, Dh = q.shape
    nb = S // BLOCK
    qb = jnp.transpose(q.reshape(B, H, nb, BLOCK, Dh), (2, 0, 1, 3, 4))
    kpos = jnp.arange(S)

    def one_block(args):
        q_blk, b = args
        z = jnp.einsum('bhqd,bhkd->bhqk', q_blk, k,
                       preferred_element_type=jnp.float32) * (HEAD_DIM ** -0.5)
        qpos = b * BLOCK + jnp.arange(BLOCK)
        causal = kpos[None, :] < qpos[:, None]
        log_beta = jax.nn.log_sigmoid(z)
        log_1m = jnp.where(causal, jax.nn.log_sigmoid(-z), 0.0)
        rem = lax.cumsum(log_1m, axis=3, reverse=True) - log_1m
        a = jnp.where(causal, jnp.exp(log_beta + rem), 0.0)
        return jnp.einsum('bhqk,bhkd->bhqd', a.astype(v.dtype), v)

    out = lax.map(one_block, (qb, jnp.arange(nb)))
    return jnp.transpose(out, (1, 2, 0, 3, 4)).reshape(B, H, S, Dh)


def hierarchical_moe(h, w_rg, b_rg, w_re, b_re, w_gate, w_up, w_down):
    B, S, D = h.shape
    t = h.reshape(B * S, D)
    g_logits = jnp.dot(t, w_rg, preferred_element_type=jnp.float32) + b_rg.astype(jnp.float32)
    g_prob = jax.nn.softmax(g_logits, axis=-1)
    g_top = jnp.argmax(g_logits, axis=-1)
    p_g = jnp.take_along_axis(g_prob, g_top[:, None], axis=-1)
    e_all = jnp.einsum('td,gde->tge', t, w_re,
                       preferred_element_type=jnp.float32) + b_re.astype(jnp.float32)
    e_logits = jnp.take_along_axis(e_all, g_top[:, None, None], axis=1)[:, 0]
    top_vals, top_idx = lax.top_k(e_logits, TOP_K_IN_GROUP)
    top_w = jax.nn.softmax(top_vals, axis=-1) * p_g
    expert_id = g_top[:, None] * EXPERTS_PER_GROUP + top_idx
    gates = jnp.sum(jax.nn.one_hot(expert_id, N_EXPERTS, dtype=jnp.float32) * top_w[..., None], axis=1)
    hid = jax.nn.silu(jnp.einsum('td,edf->tef', t, w_gate)) * jnp.einsum('td,edf->tef', t, w_up)
    y = jnp.einsum('tef,efd->td', hid * gates[..., None].astype(hid.dtype), w_down)
    return y.reshape(B, S, D)


def setup_inputs(seed: int = 0) -> dict:
    key = jax.random.key(seed)
    ks = jax.random.split(key, 16)
    f32 = jnp.float32
    nrm = lambda k, shape, s: jax.random.normal(k, shape, f32) * s
    return {
        "x": nrm(ks[0], (BATCH, SEQ, D_MODEL), 1.0),
        "norm_attn": 1.0 + nrm(ks[1], (DEPTH, D_MODEL), 0.1),
        "w_in": nrm(ks[2], (DEPTH, D_MODEL, 3 * D_MIX), D_MODEL ** -0.5),
        "rel_bias": nrm(ks[3], (N_BUCKETS, N_HEADS_DIL), 0.5),
        "out_norm_dil": 1.0 + nrm(ks[4], (DEPTH, D_DIL), 0.1),
        "out_norm_sb": 1.0 + nrm(ks[5], (DEPTH, D_SB), 0.1),
        "w_out": nrm(ks[6], (DEPTH, D_MIX, D_MODEL), D_MIX ** -0.5),
        "norm_ffn": 1.0 + nrm(ks[7], (DEPTH, D_MODEL), 0.1),
        "w_route_group": nrm(ks[8], (DEPTH, D_MODEL, N_GROUPS), D_MODEL ** -0.5),
        "b_route_group": nrm(ks[9], (DEPTH, N_GROUPS), 0.01),
        "w_route_expert": nrm(ks[10], (DEPTH, N_GROUPS, D_MODEL, EXPERTS_PER_GROUP), D_MODEL ** -0.5),
        "b_route_expert": nrm(ks[11], (DEPTH, N_GROUPS, EXPERTS_PER_GROUP), 0.01),
        "w_gate": nrm(ks[12], (DEPTH, N_EXPERTS, D_MODEL, D_EXPERT), D_MODEL ** -0.5),
        "w_up": nrm(ks[13], (DEPTH, N_EXPERTS, D_MODEL, D_EXPERT), D_MODEL ** -0.5),
        "w_down": nrm(ks[14], (DEPTH, N_EXPERTS, D_EXPERT, D_MODEL), D_EXPERT ** -0.5),
        "norm_final": 1.0 + nrm(ks[15], (D_MODEL,), 0.1),
    }


def reference(x, norm_attn, w_in, rel_bias, out_norm_dil, out_norm_sb, w_out, norm_ffn,
              w_route_group, b_route_group, w_route_expert, b_route_expert,
              w_gate, w_up, w_down, norm_final):
    B, S, D = x.shape

    def heads(t, n_heads):
        return jnp.transpose(t.reshape(B, S, n_heads, HEAD_DIM), (0, 2, 1, 3))

    def merge(t):
        return jnp.transpose(t, (0, 2, 1, 3)).reshape(B, S, -1)

    for l in range(DEPTH):
        h = rmsnorm(x, norm_attn[l])
        proj = h @ w_in[l]
        q_a, k_a, v_a, q_b, k_b, v_b = jnp.split(proj, 6, axis=-1)
        y_a = dilated_mixer(heads(q_a, N_HEADS_DIL), heads(k_a, N_HEADS_DIL),
                            heads(v_a, N_HEADS_DIL), rel_bias)
        y_b = stick_breaking_mixer(heads(q_b, N_HEADS_SB), heads(k_b, N_HEADS_SB),
                                   heads(v_b, N_HEADS_SB))
        y = jnp.concatenate([rmsnorm(merge(y_a), out_norm_dil[l]),
                             rmsnorm(merge(y_b), out_norm_sb[l])], axis=-1)
        x = x + y @ w_out[l]
        h2 = rmsnorm(x, norm_ffn[l])
        x = x + hierarchical_moe(h2, w_route_group[l], b_route_group[l], w_route_expert[l],
                                 b_route_expert[l], w_gate[l], w_up[l], w_down[l])
    return rmsnorm(x, norm_final)
```

```python
import functools
import math

import jax
import jax.numpy as jnp
from jax import lax
from jax.experimental import pallas as pl
from jax.experimental.pallas import tpu as pltpu

HEAD_DIM = 128
N_HEADS_DIL = 16
N_HEADS_SB = 16
DILATED_CONFIGS = ((128, 1), (512, 4), (2048, 16))
BLOCK = 128
N_BUCKETS = 32
MAX_DISTANCE = 2048
N_GROUPS = 4
EXPERTS_PER_GROUP = 8
N_EXPERTS = N_GROUPS * EXPERTS_PER_GROUP
D_EXPERT = 256
RMS_EPS = 1e-6
NEG_INF = -1e30

LANES = 128
VMEM_LIMIT = 56 * 1024 * 1024

F32 = jnp.float32
BF16 = jnp.bfloat16


def _params(sem, vmem=VMEM_LIMIT):
    return pltpu.CompilerParams(dimension_semantics=sem, vmem_limit_bytes=vmem)


NORM_ROWS = 64


def _norm_rows(x, g):
    var = jnp.mean(x * x, axis=-1, keepdims=True)
    return (x * lax.rsqrt(var + RMS_EPS)) * g


def _norm_matmul_kernel(*refs, n_in, has_res):
    xs = refs[:n_in]
    gs = refs[n_in:2 * n_in]
    w_ref = refs[2 * n_in]
    pos = 2 * n_in + 1
    res_ref = refs[pos] if has_res else None
    pos += int(has_res)
    o_ref, h_ref = refs[pos], refs[pos + 1]

    @pl.when(pl.program_id(1) == 0)
    def _():
        tm = h_ref.shape[0]
        off = 0
        for x_ref, g_ref in zip(xs, gs):
            k = x_ref.shape[1]

            def body(r, _, x_ref=x_ref, g_ref=g_ref, off=off, k=k):
                rows = pl.ds(pl.multiple_of(r * NORM_ROWS, NORM_ROWS), NORM_ROWS)
                h = _norm_rows(x_ref[rows, :], g_ref[...])
                h_ref[rows, off:off + k] = h.astype(BF16)
                return 0

            lax.fori_loop(0, tm // NORM_ROWS, body, 0)
            off += k

    acc = jnp.dot(h_ref[...], w_ref[...], preferred_element_type=F32)
    if has_res:
        acc = res_ref[...] + acc
    o_ref[...] = acc.astype(o_ref.dtype)


def _norm_matmul(xs, gs, w, res, out_dtype, tm, tn):
    T = xs[0].shape[0]
    K, N = w.shape
    assert sum(x.shape[1] for x in xs) == K and T % tm == 0 and N % tn == 0
    n_in = len(xs)
    in_specs = [pl.BlockSpec((tm, x.shape[1]), lambda i, j: (i, 0)) for x in xs]
    in_specs += [pl.BlockSpec((1, x.shape[1]), lambda i, j: (0, 0)) for x in xs]
    in_specs += [pl.BlockSpec((K, tn), lambda i, j: (0, j))]
    args = list(xs) + [g.reshape(1, -1).astype(F32) for g in gs] + [w]
    if res is not None:
        in_specs.append(pl.BlockSpec((tm, tn), lambda i, j: (i, j)))
        args.append(res)
    return pl.pallas_call(
        functools.partial(_norm_matmul_kernel, n_in=n_in, has_res=res is not None),
        grid=(T // tm, N // tn),
        in_specs=in_specs,
        out_specs=pl.BlockSpec((tm, tn), lambda i, j: (i, j)),
        out_shape=jax.ShapeDtypeStruct((T, N), out_dtype),
        scratch_shapes=[pltpu.VMEM((tm, K), BF16)],
        compiler_params=_params(("arbitrary", "arbitrary")),
        name="norm_matmul",
    )(*args)


def _t5_bucket(dist):
    max_exact = N_BUCKETS // 2
    d_f = jnp.maximum(dist, 1).astype(F32)
    large = max_exact + (jnp.log(d_f / max_exact) / math.log(MAX_DISTANCE / max_exact)
                         * (N_BUCKETS - max_exact)).astype(jnp.int32)
    large = jnp.minimum(large, N_BUCKETS - 1)
    return jnp.where(dist < max_exact, dist, large)


def _dilated_bias_table(rel_bias):
    qi = jnp.arange(BLOCK)[:, None]
    kj = jnp.arange(2 * BLOCK)[None, :]
    dist = BLOCK + qi - kj
    tabs = []
    for window, dilation in DILATED_CONFIGS:
        band = (dist >= 0) & (dist <= window // dilation)
        bucket = _t5_bucket(jnp.maximum(dist, 0) * dilation)
        bias = jnp.transpose(jnp.take(rel_bias, bucket, axis=0), (2, 0, 1)).astype(F32)
        tabs.append(jnp.where(band[None], bias, NEG_INF))
    return jnp.stack(tabs, axis=0)


def _strided_rows(start, dilation):
    if dilation == 1:
        return pl.ds(start, BLOCK)
    return pl.ds(start, BLOCK, stride=dilation)


def _dilated_kernel(q_ref, k_ref, v_ref, bias_ref, o_ref, q32, k32, v32, oc, lc):
    S = q_ref.shape[0]
    scale = HEAD_DIM ** -0.5
    q32[...] = q_ref[...].astype(F32)
    k32[...] = k_ref[...].astype(F32)
    v32[...] = v_ref[...].astype(F32)

    for ci, (_, dil) in enumerate(DILATED_CONFIGS):
        nb = S // dil // BLOCK

        def body(idx, _, ci=ci, dil=dil, nb=nb):
            g = idx // nb
            n = idx - g * nb
            start = g + dil * BLOCK * n
            pstart = g + dil * BLOCK * jnp.maximum(n - 1, 0)
            cur = _strided_rows(start, dil)
            prev = _strided_rows(pstart, dil)
            qs = q32[cur, :].astype(BF16)
            kc = jnp.concatenate([k32[prev, :], k32[cur, :]], axis=0).astype(BF16)
            vc = jnp.concatenate([v32[prev, :], v32[cur, :]], axis=0).astype(BF16)
            z = lax.dot_general(qs, kc, (((1,), (1,)), ((), ())),
                                preferred_element_type=F32) * scale
            bias = bias_ref[ci]
            col = lax.broadcasted_iota(jnp.int32, bias.shape, 1)
            valid = (bias > 0.5 * NEG_INF) & ((col >= BLOCK) | (n > 0))
            logits = jnp.where(valid, z + bias, NEG_INF)
            m = jnp.max(logits, axis=-1, keepdims=True)
            p = jnp.exp(logits - m)
            l = jnp.sum(p, axis=-1, keepdims=True)
            out = jnp.dot(p.astype(BF16), vc, preferred_element_type=F32) / l
            lse = m + jnp.log(l)
            oc[ci, cur, :] = out
            lc[ci, cur, :] = jnp.broadcast_to(lse, (BLOCK, HEAD_DIM))
            return 0

        lax.fori_loop(0, S // BLOCK, body, 0)

    def combine(r, _):
        rows = pl.ds(pl.multiple_of(r * BLOCK, BLOCK), BLOCK)
        l0, l1, l2 = lc[0, rows, :], lc[1, rows, :], lc[2, rows, :]
        m = jnp.maximum(jnp.maximum(l0, l1), l2)
        e0, e1, e2 = jnp.exp(l0 - m), jnp.exp(l1 - m), jnp.exp(l2 - m)
        tot = e0 + e1 + e2
        out = (e0 * oc[0, rows, :] + e1 * oc[1, rows, :] + e2 * oc[2, rows, :]) / tot
        o_ref[rows, :] = out.astype(o_ref.dtype)
        return 0

    lax.fori_loop(0, S // BLOCK, combine, 0)


def _dilated_attention(proj, bias_tab, col_q, col_k, col_v, n_heads):
    B, S, _ = proj.shape
    assert len(DILATED_CONFIGS) == 3
    assert all(S % (d * BLOCK) == 0 for _, d in DILATED_CONFIGS)

    def head_spec(col0):
        return pl.BlockSpec((None, S, HEAD_DIM), lambda b, h: (b, 0, col0 + h))

    n_cfg = len(DILATED_CONFIGS)
    return pl.pallas_call(
        _dilated_kernel,
        grid=(B, n_heads),
        in_specs=[head_spec(col_q), head_spec(col_k), head_spec(col_v),
                  pl.BlockSpec((n_cfg, None, BLOCK, 2 * BLOCK), lambda b, h: (0, h, 0, 0))],
        out_specs=pl.BlockSpec((None, S, HEAD_DIM), lambda b, h: (b, 0, h)),
        out_shape=jax.ShapeDtypeStruct((B, S, n_heads * HEAD_DIM), F32),
        scratch_shapes=[pltpu.VMEM((S, HEAD_DIM), F32)] * 3
        + [pltpu.VMEM((n_cfg, S, HEAD_DIM), F32)] * 2,
        compiler_params=_params(("arbitrary", "arbitrary")),
        name="dilated_attention",
    )(proj, proj, proj, bias_tab)


def _sb_kernel(q_ref, k_ref, v_ref, o_ref, w2_ref):
    S = q_ref.shape[0]
    scale = HEAD_DIM ** -0.5
    r2 = lax.broadcasted_iota(jnp.int32, (2 * BLOCK, 2 * BLOCK), 0) % BLOCK
    c2 = lax.broadcasted_iota(jnp.int32, (2 * BLOCK, 2 * BLOCK), 1)
    w2_ref[...] = jnp.where((c2 >= BLOCK) | (r2 > c2), 1.0, 0.0).astype(BF16)
    row = lax.broadcasted_iota(jnp.int32, (BLOCK, BLOCK), 0)
    col = lax.broadcasted_iota(jnp.int32, (BLOCK, BLOCK), 1)
    causal = col < row

    def chunk(q, c, carry, acc, diag):
        keys = pl.ds(pl.multiple_of(c * BLOCK, BLOCK), BLOCK)
        z = lax.dot_general(q, k_ref[keys, :], (((1,), (1,)), ((), ())),
                            preferred_element_type=F32) * scale
        lg = jnp.log(1.0 + jnp.exp(-jnp.abs(z)))
        log_beta = jnp.minimum(z, 0.0) - lg
        log_1m = jnp.minimum(-z, 0.0) - lg
        if diag:
            log_1m = jnp.where(causal, log_1m, 0.0)
        hi = log_1m.astype(BF16)
        lo = (log_1m - hi.astype(F32)).astype(BF16)
        rt = jnp.dot(jnp.concatenate([hi, lo], axis=1), w2_ref[...],
                     preferred_element_type=F32)
        a = jnp.exp(log_beta + carry + rt[:, :BLOCK])
        if diag:
            a = jnp.where(causal, a, 0.0)
        acc = acc + jnp.dot(a.astype(BF16), v_ref[keys, :], preferred_element_type=F32)
        return carry + rt[:, BLOCK:], acc

    def q_block(i, _):
        rows = pl.ds(pl.multiple_of(i * BLOCK, BLOCK), BLOCK)
        q = q_ref[rows, :]
        zeros = jnp.zeros((BLOCK, HEAD_DIM), F32)
        carry, acc = chunk(q, i, zeros, zeros, True)

        def body(t, st):
            return chunk(q, i - 1 - t, st[0], st[1], False)

        carry, acc = lax.fori_loop(0, i, body, (carry, acc))
        o_ref[rows, :] = acc.astype(o_ref.dtype)
        return 0

    lax.fori_loop(0, S // BLOCK, q_block, 0)


def _sb_attention(proj, col_q, col_k, col_v, n_heads):
    B, S, _ = proj.shape
    assert S % BLOCK == 0 and HEAD_DIM == BLOCK

    def head_spec(col0):
        return pl.BlockSpec((None, S, HEAD_DIM), lambda b, h: (b, 0, col0 + h))

    return pl.pallas_call(
        _sb_kernel,
        grid=(B, n_heads),
        in_specs=[head_spec(col_q), head_spec(col_k), head_spec(col_v)],
        out_specs=pl.BlockSpec((None, S, HEAD_DIM), lambda b, h: (b, 0, h)),
        out_shape=jax.ShapeDtypeStruct((B, S, n_heads * HEAD_DIM), F32),
        scratch_shapes=[pltpu.VMEM((2 * BLOCK, 2 * BLOCK), BF16)],
        compiler_params=_params(("arbitrary", "arbitrary")),
        name="sb_attention",
    )(proj, proj, proj)


def _router_kernel(x_ref, g_ref, wr_ref, br_ref, id_ref, wt_ref):
    h = _norm_rows(x_ref[...], g_ref[...])
    logits = jnp.dot(h, wr_ref[...], precision=lax.Precision.HIGHEST,
                     preferred_element_type=F32) + br_ref[...]
    lane = lax.broadcasted_iota(jnp.int32, logits.shape, 1).astype(F32)
    big = float(LANES)
    ninf = -jnp.inf

    def first_argmax(vals):
        top = jnp.max(vals, axis=-1, keepdims=True)
        idx = jnp.min(jnp.where(vals == top, lane, big), axis=-1, keepdims=True)
        return top, idx

    gl = jnp.where(lane < N_GROUPS, logits, ninf)
    gmax, g_top = first_argmax(gl)
    p_g = 1.0 / jnp.sum(jnp.exp(gl - gmax), axis=-1, keepdims=True)
    first = N_GROUPS + g_top * EXPERTS_PER_GROUP
    el = jnp.where((lane >= first) & (lane < first + EXPERTS_PER_GROUP), logits, ninf)
    v1, i1 = first_argmax(el)
    v2, i2 = first_argmax(jnp.where(lane == i1, ninf, el))
    t = jnp.exp(v2 - v1)
    w1 = p_g / (1.0 + t)
    w2 = p_g * t / (1.0 + t)
    ids = jnp.where(lane == 0, i1 - N_GROUPS, jnp.where(lane == 1, i2 - N_GROUPS, 0.0))
    id_ref[...] = ids.astype(jnp.int32)
    wt_ref[...] = jnp.where(lane == 0, w1, jnp.where(lane == 1, w2, 0.0))


def _router(x, g, w_rg, b_rg, w_re, b_re, tm):
    T, D = x.shape
    n_logits = N_GROUPS + N_EXPERTS
    wr = jnp.concatenate([w_rg, jnp.transpose(w_re, (1, 0, 2)).reshape(D, N_EXPERTS)], axis=1)
    wr = jnp.pad(wr.astype(F32), ((0, 0), (0, LANES - n_logits)))
    br = jnp.pad(jnp.concatenate([b_rg, b_re.reshape(-1)]).astype(F32), (0, LANES - n_logits))
    return pl.pallas_call(
        _router_kernel,
        grid=(T // tm,),
        in_specs=[pl.BlockSpec((tm, D), lambda i: (i, 0)),
                  pl.BlockSpec((1, D), lambda i: (0, 0)),
                  pl.BlockSpec((D, LANES), lambda i: (0, 0)),
                  pl.BlockSpec((1, LANES), lambda i: (0, 0))],
        out_specs=[pl.BlockSpec((tm, LANES), lambda i: (i, 0))] * 2,
        out_shape=[jax.ShapeDtypeStruct((T, LANES), jnp.int32),
                   jax.ShapeDtypeStruct((T, LANES), F32)],
        compiler_params=_params(("arbitrary",)),
        name="router",
    )(x, g.reshape(1, D).astype(F32), wr, br.reshape(1, LANES))


def _dispatch_plan(eid, wts, tm, n_tiles):
    T = eid.shape[0]
    n_assign = 2 * T
    e_flat = eid.T.reshape(-1)
    w_flat = wts.T.reshape(-1)
    order = jnp.argsort(e_flat, stable=True).astype(jnp.int32)
    experts = jnp.arange(N_EXPERTS, dtype=jnp.int32)
    counts = jnp.sum((e_flat[:, None] == experts[None, :]).astype(jnp.int32), axis=0)
    tiles_per = (counts + tm - 1) // tm
    tile_end = jnp.cumsum(tiles_per)
    tile_start = tile_end - tiles_per
    cstart = jnp.cumsum(counts) - counts
    tile_idx = jnp.arange(n_tiles, dtype=jnp.int32)
    tile_expert = jnp.minimum(
        jnp.sum((tile_end[None, :] <= tile_idx[:, None]).astype(jnp.int32), axis=1), N_EXPERTS - 1)
    local_tile = tile_idx - tile_start[tile_expert]
    tile_nvalid = jnp.clip(counts[tile_expert] - local_tile * tm, 0, tm)
    tile_nvalid = jnp.where(tile_idx < tile_end[-1], tile_nvalid, 0).astype(jnp.int32)

    p = jnp.arange(n_tiles * tm, dtype=jnp.int32)
    pt = p // tm
    valid = (p - pt * tm) < tile_nvalid[pt]
    pe = tile_expert[pt]
    sidx = jnp.clip(cstart[pe] + (p - tile_start[pe] * tm), 0, n_assign - 1)
    a = order[sidx]
    row_src = jnp.where(valid, a % T, 0).astype(jnp.int32)
    row_dst = jnp.where(valid, a, 0).astype(jnp.int32)
    row_w = jnp.where(valid, w_flat[a], 0.0).astype(F32)
    return tile_expert.astype(jnp.int32), tile_nvalid, row_src, row_dst, row_w


def _ffn_kernel(texp_ref, nval_ref, src_ref, dst_ref, roww_ref, g_ref, wgu_ref, wd_ref,
                x_hbm, o_hbm, xbuf, obuf, gsem, ssem):
    del texp_ref
    i = pl.program_id(0)
    tm = xbuf.shape[0]
    f = wd_ref.shape[0]

    def gather_copy(r):
        return pltpu.make_async_copy(x_hbm.at[pl.ds(src_ref[0, 0, r], 1)],
                                     xbuf.at[pl.ds(r, 1)], gsem)

    def scatter_copy(r):
        return pltpu.make_async_copy(obuf.at[pl.ds(r, 1)],
                                     o_hbm.at[pl.ds(dst_ref[0, 0, r], 1)], ssem)

    nval = nval_ref[i]

    def for_rows(fn):
        def body(r, _):
            fn(r)
            return 0
        lax.fori_loop(0, nval, body, 0)

    @pl.when(i == 0)
    def _():
        xbuf[...] = jnp.zeros_like(xbuf)

    @pl.when(nval > 0)
    def _():
        for_rows(lambda r: gather_copy(r).start())
        for_rows(lambda r: gather_copy(r).wait())

        def norm(r, _):
            rows = pl.ds(pl.multiple_of(r * NORM_ROWS, NORM_ROWS), NORM_ROWS)
            xbuf[rows, :] = _norm_rows(xbuf[rows, :], g_ref[...])
            return 0

        lax.fori_loop(0, tm // NORM_ROWS, norm, 0)
        h = jnp.dot(xbuf[...].astype(BF16), wgu_ref[...], preferred_element_type=F32)
        hg, hu = h[:, :f], h[:, f:]
        hid = (hg * (1.0 / (1.0 + jnp.exp(-hg)))) * hu * roww_ref[...]
        obuf[...] = jnp.dot(hid.astype(BF16), wd_ref[...], preferred_element_type=F32)
        for_rows(lambda r: scatter_copy(r).start())
        for_rows(lambda r: scatter_copy(r).wait())


def _moe_ffn(x, g, plan, wgu, wd, tm, n_tiles):
    T, D = x.shape
    tile_expert, tile_nvalid, row_src, row_dst, row_w = plan
    R = n_tiles * tm
    f2 = wgu.shape[2]
    grid_spec = pltpu.PrefetchScalarGridSpec(
        num_scalar_prefetch=2,
        grid=(n_tiles,),
        in_specs=[
            pl.BlockSpec((1, 1, tm), lambda i, te, nv: (i, 0, 0), memory_space=pltpu.SMEM),
            pl.BlockSpec((1, 1, tm), lambda i, te, nv: (i, 0, 0), memory_space=pltpu.SMEM),
            pl.BlockSpec((tm, 1), lambda i, te, nv: (i, 0)),
            pl.BlockSpec((1, D), lambda i, te, nv: (0, 0)),
            pl.BlockSpec((None, D, f2), lambda i, te, nv: (te[i], 0, 0)),
            pl.BlockSpec((None, f2 // 2, D), lambda i, te, nv: (te[i], 0, 0)),
            pl.BlockSpec(memory_space=pl.ANY),
        ],
        out_specs=pl.BlockSpec(memory_space=pl.ANY),
        scratch_shapes=[pltpu.VMEM((tm, D), F32), pltpu.VMEM((tm, D), F32),
                        pltpu.SemaphoreType.DMA(()), pltpu.SemaphoreType.DMA(())],
    )
    return pl.pallas_call(
        _ffn_kernel,
        grid_spec=grid_spec,
        out_shape=jax.ShapeDtypeStruct((2 * T, D), F32),
        compiler_params=_params(("arbitrary",)),
        name="moe_ffn",
    )(tile_expert, tile_nvalid, row_src.reshape(n_tiles, 1, tm), row_dst.reshape(n_tiles, 1, tm),
      row_w.reshape(R, 1), g.reshape(1, D).astype(F32), wgu, wd, x)


def _combine_kernel(x_ref, o0_ref, o1_ref, g_ref, y_ref, *, final_norm):
    y = x_ref[...] + (o0_ref[...] + o1_ref[...])
    if final_norm:
        y = _norm_rows(y, g_ref[...])
    y_ref[...] = y


def _combine(x, o, g, final_norm, tm):
    T, D = x.shape
    nt = T // tm
    return pl.pallas_call(
        functools.partial(_combine_kernel, final_norm=final_norm),
        grid=(nt,),
        in_specs=[pl.BlockSpec((tm, D), lambda i: (i, 0)),
                  pl.BlockSpec((tm, D), lambda i: (i, 0)),
                  pl.BlockSpec((tm, D), lambda i: (i + nt, 0)),
                  pl.BlockSpec((1, D), lambda i: (0, 0))],
        out_specs=pl.BlockSpec((tm, D), lambda i: (i, 0)),
        out_shape=jax.ShapeDtypeStruct((T, D), F32),
        compiler_params=_params(("arbitrary",)),
        name="combine",
    )(x, o, o, g.reshape(1, D).astype(F32))


PROJ_TM, PROJ_TN = 512, 1024
OUT_TM, OUT_TN = 512, 512
ROUTER_TM = 256
FFN_TM = 256
COMBINE_TM = 256


def kernel(x, norm_attn, w_in, rel_bias, out_norm_dil, out_norm_sb, w_out, norm_ffn,
           w_route_group, b_route_group, w_route_expert, b_route_expert,
           w_gate, w_up, w_down, norm_final):
    B, S, D = x.shape
    T = B * S
    depth = w_in.shape[0]
    n_tiles = 2 * T // FFN_TM + N_EXPERTS
    bias_tab = _dilated_bias_table(rel_bias)
    xt = x.reshape(T, D)
    hd, hs = N_HEADS_DIL, N_HEADS_SB

    for l in range(depth):
        proj = _norm_matmul([xt], [norm_attn[l]], w_in[l].astype(BF16), None, BF16,
                            PROJ_TM, PROJ_TN).reshape(B, S, -1)
        y_a = _dilated_attention(proj, bias_tab, 0, hd, 2 * hd, hd)
        y_b = _sb_attention(proj, 3 * hd, 3 * hd + hs, 3 * hd + 2 * hs, hs)
        xt = _norm_matmul([y_a.reshape(T, -1), y_b.reshape(T, -1)],
                          [out_norm_dil[l], out_norm_sb[l]], w_out[l].astype(BF16), xt, F32,
                          OUT_TM, OUT_TN)
        eid, wts = _router(xt, norm_ffn[l], w_route_group[l], b_route_group[l],
                           w_route_expert[l], b_route_expert[l], ROUTER_TM)
        plan = _dispatch_plan(eid[:, :2], wts[:, :2], FFN_TM, n_tiles)
        wgu = jnp.concatenate([w_gate[l], w_up[l]], axis=-1).astype(BF16)
        o = _moe_ffn(xt, norm_ffn[l], plan, wgu, w_down[l].astype(BF16), FFN_TM, n_tiles)
        xt = _combine(xt, o, norm_final, l == depth - 1, COMBINE_TM)
    return xt.reshape(B, S, D)
```

```python
import functools
import math

import jax
import jax.numpy as jnp
from jax import lax
from jax.experimental import pallas as pl
from jax.experimental.pallas import tpu as pltpu

HEAD_DIM = 128
N_HEADS_DIL = 16
N_HEADS_SB = 16
DILATED_CONFIGS = ((128, 1), (512, 4), (2048, 16))
BLOCK = 128
N_BUCKETS = 32
MAX_DISTANCE = 2048
N_GROUPS = 4
EXPERTS_PER_GROUP = 8
N_EXPERTS = N_GROUPS * EXPERTS_PER_GROUP
D_EXPERT = 256
RMS_EPS = 1e-6
NEG_INF = -1e30

LANES = 128
VMEM_LIMIT = 56 * 1024 * 1024

F32 = jnp.float32
BF16 = jnp.bfloat16


def _params(sem, vmem=VMEM_LIMIT):
    return pltpu.CompilerParams(dimension_semantics=sem, vmem_limit_bytes=vmem)


NORM_ROWS = 64


def _norm_rows(x, g):
    var = jnp.mean(x * x, axis=-1, keepdims=True)
    return (x * lax.rsqrt(var + RMS_EPS)) * g


def _norm_matmul_kernel(*refs, n_in, has_res):
    xs = refs[:n_in]
    gs = refs[n_in:2 * n_in]
    w_ref = refs[2 * n_in]
    pos = 2 * n_in + 1
    res_ref = refs[pos] if has_res else None
    pos += int(has_res)
    o_ref, h_ref = refs[pos], refs[pos + 1]

    @pl.when(pl.program_id(1) == 0)
    def _():
        tm = h_ref.shape[0]
        off = 0
        for x_ref, g_ref in zip(xs, gs):
            k = x_ref.shape[1]

            def body(r, _, x_ref=x_ref, g_ref=g_ref, off=off, k=k):
                rows = pl.ds(pl.multiple_of(r * NORM_ROWS, NORM_ROWS), NORM_ROWS)
                h = _norm_rows(x_ref[rows, :], g_ref[...])
                h_ref[rows, off:off + k] = h.astype(BF16)
                return 0

            lax.fori_loop(0, tm // NORM_ROWS, body, 0)
            off += k

    acc = jnp.dot(h_ref[...], w_ref[...], preferred_element_type=F32)
    if has_res:
        acc = res_ref[...] + acc
    o_ref[...] = acc.astype(o_ref.dtype)


def _norm_matmul(xs, gs, w, res, out_dtype, tm, tn):
    T = xs[0].shape[0]
    K, N = w.shape
    assert sum(x.shape[1] for x in xs) == K and T % tm == 0 and N % tn == 0
    n_in = len(xs)
    in_specs = [pl.BlockSpec((tm, x.shape[1]), lambda i, j: (i, 0)) for x in xs]
    in_specs += [pl.BlockSpec((1, x.shape[1]), lambda i, j: (0, 0)) for x in xs]
    in_specs += [pl.BlockSpec((K, tn), lambda i, j: (0, j))]
    args = list(xs) + [g.reshape(1, -1).astype(F32) for g in gs] + [w]
    if res is not None:
        in_specs.append(pl.BlockSpec((tm, tn), lambda i, j: (i, j)))
        args.append(res)
    return pl.pallas_call(
        functools.partial(_norm_matmul_kernel, n_in=n_in, has_res=res is not None),
        grid=(T // tm, N // tn),
        in_specs=in_specs,
        out_specs=pl.BlockSpec((tm, tn), lambda i, j: (i, j)),
        out_shape=jax.ShapeDtypeStruct((T, N), out_dtype),
        scratch_shapes=[pltpu.VMEM((tm, K), BF16)],
        compiler_params=_params(("arbitrary", "arbitrary")),
        name="norm_matmul",
    )(*args)


def _t5_bucket(dist):
    max_exact = N_BUCKETS // 2
    d_f = jnp.maximum(dist, 1).astype(F32)
    large = max_exact + (jnp.log(d_f / max_exact) / math.log(MAX_DISTANCE / max_exact)
                         * (N_BUCKETS - max_exact)).astype(jnp.int32)
    large = jnp.minimum(large, N_BUCKETS - 1)
    return jnp.where(dist < max_exact, dist, large)


def _dilated_bias_table(rel_bias):
    qi = jnp.arange(BLOCK)[:, None]
    kj = jnp.arange(2 * BLOCK)[None, :]
    dist = BLOCK + qi - kj
    tabs = []
    for window, dilation in DILATED_CONFIGS:
        band = (dist >= 0) & (dist <= window // dilation)
        bucket = _t5_bucket(jnp.maximum(dist, 0) * dilation)
        bias = jnp.transpose(jnp.take(rel_bias, bucket, axis=0), (2, 0, 1)).astype(F32)
        tabs.append(jnp.where(band[None], bias, NEG_INF))
    return jnp.stack(tabs, axis=0)


DIL_UNROLL = 8


def _strided_rows(start, dilation):
    if dilation == 1:
        return pl.ds(start, BLOCK)
    return pl.ds(start, BLOCK, stride=dilation)


def _dilated_kernel(q_ref, k_ref, v_ref, bias_ref, o_ref, q32, k32, v32, oc, lc):
    S = q_ref.shape[0]
    scale = HEAD_DIM ** -0.5
    q32[...] = q_ref[...].astype(F32)
    k32[...] = k_ref[...].astype(F32)
    v32[...] = v_ref[...].astype(F32)

    for ci, (_, dil) in enumerate(DILATED_CONFIGS):
        nb = S // dil // BLOCK

        def block(idx, ci=ci, dil=dil, nb=nb):
            g = idx // nb
            n = idx - g * nb
            start = g + dil * BLOCK * n
            pstart = g + dil * BLOCK * jnp.maximum(n - 1, 0)
            cur = _strided_rows(start, dil)
            prev = _strided_rows(pstart, dil)
            qs = q32[cur, :].astype(BF16)
            kc = jnp.concatenate([k32[prev, :], k32[cur, :]], axis=0).astype(BF16)
            vc = jnp.concatenate([v32[prev, :], v32[cur, :]], axis=0).astype(BF16)
            z = lax.dot_general(qs, kc, (((1,), (1,)), ((), ())),
                                preferred_element_type=F32) * scale
            bias = bias_ref[ci]
            col = lax.broadcasted_iota(jnp.int32, bias.shape, 1)
            valid = (bias > 0.5 * NEG_INF) & ((col >= BLOCK) | (n > 0))
            logits = jnp.where(valid, z + bias, NEG_INF)
            m = jnp.max(logits, axis=-1, keepdims=True)
            p = jnp.exp(logits - m)
            l = jnp.sum(p, axis=-1, keepdims=True)
            out = jnp.dot(p.astype(BF16), vc, preferred_element_type=F32) / l
            lse = m + jnp.log(l)
            oc[ci, cur, :] = out
            lc[ci, cur, :] = jnp.broadcast_to(lse, (BLOCK, HEAD_DIM))

        def body(t, _, block=block):
            for u in range(DIL_UNROLL):
                block(t * DIL_UNROLL + u)
            return 0

        lax.fori_loop(0, S // BLOCK // DIL_UNROLL, body, 0)

    def combine(r, _):
        rows = pl.ds(pl.multiple_of(r * BLOCK, BLOCK), BLOCK)
        l0, l1, l2 = lc[0, rows, :], lc[1, rows, :], lc[2, rows, :]
        m = jnp.maximum(jnp.maximum(l0, l1), l2)
        e0, e1, e2 = jnp.exp(l0 - m), jnp.exp(l1 - m), jnp.exp(l2 - m)
        tot = e0 + e1 + e2
        out = (e0 * oc[0, rows, :] + e1 * oc[1, rows, :] + e2 * oc[2, rows, :]) / tot
        o_ref[rows, :] = out.astype(o_ref.dtype)
        return 0

    lax.fori_loop(0, S // BLOCK, combine, 0)


def _dilated_attention(proj, bias_tab, col_q, col_k, col_v, n_heads):
    B, S, _ = proj.shape
    assert len(DILATED_CONFIGS) == 3
    assert all(S % (d * BLOCK) == 0 for _, d in DILATED_CONFIGS)
    assert (S // BLOCK) % DIL_UNROLL == 0

    def head_spec(col0):
        return pl.BlockSpec((None, S, HEAD_DIM), lambda b, h: (b, 0, col0 + h))

    n_cfg = len(DILATED_CONFIGS)
    return pl.pallas_call(
        _dilated_kernel,
        grid=(B, n_heads),
        in_specs=[head_spec(col_q), head_spec(col_k), head_spec(col_v),
                  pl.BlockSpec((n_cfg, None, BLOCK, 2 * BLOCK), lambda b, h: (0, h, 0, 0))],
        out_specs=pl.BlockSpec((None, S, HEAD_DIM), lambda b, h: (b, 0, h)),
        out_shape=jax.ShapeDtypeStruct((B, S, n_heads * HEAD_DIM), F32),
        scratch_shapes=[pltpu.VMEM((S, HEAD_DIM), F32)] * 3
        + [pltpu.VMEM((n_cfg, S, HEAD_DIM), F32)] * 2,
        compiler_params=_params(("arbitrary", "arbitrary")),
        name="dilated_attention",
    )(proj, proj, proj, bias_tab)


SB_TQ = 512
SB_UNROLL = 4


def _sb_kernel(q_ref, k_ref, v_ref, o_ref, w2_ref, acc_ref, carry_ref):
    S = q_ref.shape[0]
    nsub = SB_TQ // BLOCK
    scale = HEAD_DIM ** -0.5
    r2 = lax.broadcasted_iota(jnp.int32, (BLOCK, 2 * BLOCK), 0)
    c2 = lax.broadcasted_iota(jnp.int32, (BLOCK, 2 * BLOCK), 1)
    w2_ref[...] = jnp.where((c2 >= BLOCK) | (r2 > c2), 1.0, 0.0).astype(BF16)

    def chunk(qbase, r0, kstart, diag):
        nrow = SB_TQ - r0
        keys = pl.ds(kstart, BLOCK)
        q = q_ref[pl.ds(pl.multiple_of(qbase + r0, BLOCK), nrow), :]
        z = lax.dot_general(q, k_ref[keys, :], (((1,), (1,)), ((), ())),
                            preferred_element_type=F32) * scale
        lg = jnp.log(1.0 + jnp.exp(-jnp.abs(z)))
        log_beta = jnp.minimum(z, 0.0) - lg
        log_1m = log_beta - z
        if diag:
            causal = (lax.broadcasted_iota(jnp.int32, (nrow, BLOCK), 1)
                      < lax.broadcasted_iota(jnp.int32, (nrow, BLOCK), 0))
            log_1m = jnp.where(causal, log_1m, 0.0)
        rt = jnp.dot(log_1m.astype(BF16), w2_ref[...],
                     preferred_element_type=F32)
        a = jnp.exp(log_beta + carry_ref[r0:, :] + rt[:, :BLOCK])
        if diag:
            a = jnp.where(causal, a, 0.0)
        acc_ref[r0:, :] += jnp.dot(a.astype(BF16), v_ref[keys, :], preferred_element_type=F32)
        carry_ref[r0:, :] += rt[:, BLOCK:]

    def q_tile(i, _):
        qbase = pl.multiple_of(i * SB_TQ, SB_TQ)
        acc_ref[...] = jnp.zeros_like(acc_ref)
        carry_ref[...] = jnp.zeros_like(carry_ref)
        for j in reversed(range(nsub)):
            chunk(qbase, j * BLOCK, pl.multiple_of(qbase + j * BLOCK, BLOCK), True)

        def body(t, _):
            for u in range(SB_UNROLL):
                c = i * nsub - 1 - (t * SB_UNROLL + u)
                chunk(qbase, 0, pl.multiple_of(c * BLOCK, BLOCK), False)
            return 0

        lax.fori_loop(0, i * (nsub // SB_UNROLL), body, 0)
        o_ref[pl.ds(qbase, SB_TQ), :] = acc_ref[...].astype(o_ref.dtype)
        return 0

    lax.fori_loop(0, S // SB_TQ, q_tile, 0)


def _sb_attention(proj, col_q, col_k, col_v, n_heads):
    B, S, _ = proj.shape
    assert S % SB_TQ == 0 and HEAD_DIM == BLOCK and (SB_TQ // BLOCK) % SB_UNROLL == 0

    def head_spec(col0):
        return pl.BlockSpec((None, S, HEAD_DIM), lambda b, h: (b, 0, col0 + h))

    return pl.pallas_call(
        _sb_kernel,
        grid=(B, n_heads),
        in_specs=[head_spec(col_q), head_spec(col_k), head_spec(col_v)],
        out_specs=pl.BlockSpec((None, S, HEAD_DIM), lambda b, h: (b, 0, h)),
        out_shape=jax.ShapeDtypeStruct((B, S, n_heads * HEAD_DIM), F32),
        scratch_shapes=[pltpu.VMEM((BLOCK, 2 * BLOCK), BF16),
                        pltpu.VMEM((SB_TQ, HEAD_DIM), F32), pltpu.VMEM((SB_TQ, HEAD_DIM), F32)],
        compiler_params=_params(("arbitrary", "arbitrary")),
        name="sb_attention",
    )(proj, proj, proj)


def _router_kernel(x_ref, g_ref, wr_ref, br_ref, id_ref, wt_ref, cnt_ref):
    h = _norm_rows(x_ref[...], g_ref[...])
    logits = jnp.dot(h, wr_ref[...], precision=lax.Precision.HIGHEST,
                     preferred_element_type=F32) + br_ref[...]
    lane = lax.broadcasted_iota(jnp.int32, logits.shape, 1).astype(F32)
    big = float(LANES)
    ninf = -jnp.inf

    def first_argmax(vals):
        top = jnp.max(vals, axis=-1, keepdims=True)
        idx = jnp.min(jnp.where(vals == top, lane, big), axis=-1, keepdims=True)
        return top, idx

    gl = jnp.where(lane < N_GROUPS, logits, ninf)
    gmax, g_top = first_argmax(gl)
    p_g = 1.0 / jnp.sum(jnp.exp(gl - gmax), axis=-1, keepdims=True)
    first = N_GROUPS + g_top * EXPERTS_PER_GROUP
    el = jnp.where((lane >= first) & (lane < first + EXPERTS_PER_GROUP), logits, ninf)
    v1, i1 = first_argmax(el)
    v2, i2 = first_argmax(jnp.where(lane == i1, ninf, el))
    t = jnp.exp(v2 - v1)
    w1 = p_g / (1.0 + t)
    w2 = p_g * t / (1.0 + t)
    ids = jnp.where(lane == 0, i1 - N_GROUPS, jnp.where(lane == 1, i2 - N_GROUPS, 0.0))
    id_ref[...] = ids.astype(jnp.int32)
    wt_ref[...] = jnp.where(lane == 0, w1, jnp.where(lane == 1, w2, 0.0))

    @pl.when(pl.program_id(0) == 0)
    def _():
        cnt_ref[...] = jnp.zeros_like(cnt_ref)

    chosen = jnp.where((lane == i1) | (lane == i2), 1.0, 0.0)
    cnt_ref[...] += jnp.sum(chosen, axis=0, keepdims=True)


def _router(x, g, w_rg, b_rg, w_re, b_re, tm):
    T, D = x.shape
    n_logits = N_GROUPS + N_EXPERTS
    wr = jnp.concatenate([w_rg, jnp.transpose(w_re, (1, 0, 2)).reshape(D, N_EXPERTS)], axis=1)
    wr = jnp.pad(wr.astype(F32), ((0, 0), (0, LANES - n_logits)))
    br = jnp.pad(jnp.concatenate([b_rg, b_re.reshape(-1)]).astype(F32), (0, LANES - n_logits))
    eid, wts, cnt = pl.pallas_call(
        _router_kernel,
        grid=(T // tm,),
        in_specs=[pl.BlockSpec((tm, D), lambda i: (i, 0)),
                  pl.BlockSpec((1, D), lambda i: (0, 0)),
                  pl.BlockSpec((D, LANES), lambda i: (0, 0)),
                  pl.BlockSpec((1, LANES), lambda i: (0, 0))],
        out_specs=[pl.BlockSpec((tm, LANES), lambda i: (i, 0))] * 2
        + [pl.BlockSpec((1, LANES), lambda i: (0, 0))],
        out_shape=[jax.ShapeDtypeStruct((T, LANES), jnp.int32),
                   jax.ShapeDtypeStruct((T, LANES), F32),
                   jax.ShapeDtypeStruct((1, LANES), F32)],
        compiler_params=_params(("arbitrary",)),
        name="router",
    )(x, g.reshape(1, D).astype(F32), wr, br.reshape(1, LANES))
    counts = cnt[0, N_GROUPS:N_GROUPS + N_EXPERTS].astype(jnp.int32)
    return eid[:, :2], wts, counts


def _dispatch_plan(eid, counts, tm, n_tiles):
    T = eid.shape[0]
    n_assign = 2 * T
    order = jnp.argsort(eid.T.reshape(-1), stable=True).astype(jnp.int32)
    tiles_per = (counts + tm - 1) // tm
    tile_end = jnp.cumsum(tiles_per)
    tile_idx = jnp.arange(n_tiles, dtype=jnp.int32)
    tile_expert = jnp.minimum(
        jnp.sum((tile_end[None, :] <= tile_idx[:, None]).astype(jnp.int32), axis=1), N_EXPERTS - 1)
    onehot = (tile_expert[:, None] == jnp.arange(N_EXPERTS, dtype=jnp.int32)[None, :])

    def of_tile(per_expert):
        return jnp.sum(jnp.where(onehot, per_expert[None, :], 0), axis=1)

    local = (tile_idx - of_tile(tile_end - tiles_per)) * tm
    tile_nvalid = jnp.where(tile_idx < tile_end[-1], jnp.clip(of_tile(counts) - local, 0, tm), 0)
    tile_base = jnp.clip(of_tile(jnp.cumsum(counts) - counts) + local, 0, n_assign)
    order_ext = jnp.concatenate([order, jnp.zeros((tm,), jnp.int32)])
    ids = jax.vmap(lambda b: lax.dynamic_slice(order_ext, (b,), (tm,)))(tile_base)
    row_src = jnp.where(ids >= T, ids - T, ids)
    return tile_expert, tile_nvalid.astype(jnp.int32), row_src, ids


DMA_ISSUE_UNROLL = 8


def _for_rows(n, fn):
    n_groups = n // DMA_ISSUE_UNROLL

    def group(gi, _):
        for u in range(DMA_ISSUE_UNROLL):
            fn(gi * DMA_ISSUE_UNROLL + u)
        return 0

    def single(r, _):
        fn(r)
        return 0

    lax.fori_loop(0, n_groups, group, 0)
    lax.fori_loop(n_groups * DMA_ISSUE_UNROLL, n, single, 0)


def _wait_rows(n, max_rows, copy_of_rows):
    k = max_rows
    while k >= 1:
        @pl.when((n & k) != 0)
        def _(k=k):
            copy_of_rows(k).wait()
        k //= 2


def _ffn_kernel(texp_ref, nval_ref, src_ref, src_next_ref, dst_ref, g_ref, wgu_ref, wd_ref,
                x_hbm, o_hbm, xbuf, hbuf, obuf, gsem, ssem):
    del texp_ref
    i = pl.program_id(0)
    n_tiles = pl.num_programs(0)
    tm = xbuf.shape[1]
    f = wd_ref.shape[0]
    slot = i % 2
    other = 1 - slot
    nval = nval_ref[i]
    nval_next = jnp.where(i + 1 < n_tiles, nval_ref[jnp.minimum(i + 1, n_tiles - 1)], 0)

    def start_gather(idx_ref, s, n):
        _for_rows(n, lambda r: pltpu.make_async_copy(
            x_hbm.at[pl.ds(idx_ref[0, 0, r], 1)], xbuf.at[s, pl.ds(r, 1)], gsem.at[s]).start())

    def wait_gather(s, n):
        _wait_rows(n, tm, lambda k: pltpu.make_async_copy(
            x_hbm.at[pl.ds(0, k)], xbuf.at[s, pl.ds(0, k)], gsem.at[s]))

    def wait_scatter(s, n):
        _wait_rows(n, tm, lambda k: pltpu.make_async_copy(
            obuf.at[s, pl.ds(0, k)], o_hbm.at[pl.ds(0, k)], ssem.at[s]))

    @pl.when(i == 0)
    def _():
        xbuf[...] = jnp.zeros_like(xbuf)
        start_gather(src_ref, 0, nval)

    @pl.when(nval > 0)
    def _():
        start_gather(src_next_ref, other, nval_next)
        wait_gather(slot, nval)

        @pl.when(i >= 2)
        def _():
            wait_scatter(slot, nval_ref[jnp.maximum(i - 2, 0)])

        def norm(r, _):
            rows = pl.ds(pl.multiple_of(r * NORM_ROWS, NORM_ROWS), NORM_ROWS)
            hbuf[rows, :] = _norm_rows(xbuf[slot, rows, :], g_ref[...]).astype(BF16)
            return 0

        lax.fori_loop(0, tm // NORM_ROWS, norm, 0)
        h = jnp.dot(hbuf[...], wgu_ref[...], preferred_element_type=F32)
        hg, hu = h[:, :f], h[:, f:]
        hid = (hg * (1.0 / (1.0 + jnp.exp(-hg)))) * hu
        obuf[slot] = jnp.dot(hid.astype(BF16), wd_ref[...], preferred_element_type=F32)
        _for_rows(nval, lambda r: pltpu.make_async_copy(
            obuf.at[slot, pl.ds(r, 1)], o_hbm.at[pl.ds(dst_ref[0, 0, r], 1)],
            ssem.at[slot]).start())

        @pl.when(nval_next == 0)
        def _():
            @pl.when(i >= 1)
            def _():
                wait_scatter(other, nval_ref[jnp.maximum(i - 1, 0)])

            wait_scatter(slot, nval)


def _moe_ffn(x, g, plan, wgu, wd, tm, n_tiles):
    T, D = x.shape
    tile_expert, tile_nvalid, row_src, row_dst = plan
    f2 = wgu.shape[2]
    row_src = row_src.reshape(n_tiles, 1, tm)

    def idx_spec(step):
        return pl.BlockSpec((1, 1, tm),
                            lambda i, te, nv: (jnp.minimum(i + step, n_tiles - 1), 0, 0),
                            memory_space=pltpu.SMEM)

    grid_spec = pltpu.PrefetchScalarGridSpec(
        num_scalar_prefetch=2,
        grid=(n_tiles,),
        in_specs=[
            idx_spec(0), idx_spec(1), idx_spec(0),
            pl.BlockSpec((1, D), lambda i, te, nv: (0, 0)),
            pl.BlockSpec((None, D, f2), lambda i, te, nv: (te[i], 0, 0)),
            pl.BlockSpec((None, f2 // 2, D), lambda i, te, nv: (te[i], 0, 0)),
            pl.BlockSpec(memory_space=pl.ANY),
        ],
        out_specs=pl.BlockSpec(memory_space=pl.ANY),
        scratch_shapes=[pltpu.VMEM((2, tm, D), F32), pltpu.VMEM((tm, D), BF16),
                        pltpu.VMEM((2, tm, D), F32),
                        pltpu.SemaphoreType.DMA((2,)), pltpu.SemaphoreType.DMA((2,))],
    )
    return pl.pallas_call(
        _ffn_kernel,
        grid_spec=grid_spec,
        out_shape=jax.ShapeDtypeStruct((2 * T, D), F32),
        compiler_params=_params(("arbitrary",)),
        name="moe_ffn",
    )(tile_expert, tile_nvalid, row_src, row_src, row_dst.reshape(n_tiles, 1, tm),
      g.reshape(1, D).astype(F32), wgu, wd, x)


def _combine_kernel(x_ref, o0_ref, o1_ref, wt_ref, g_ref, y_ref, *, final_norm):
    wt = wt_ref[...]
    y = x_ref[...] + (wt[:, 0:1] * o0_ref[...] + wt[:, 1:2] * o1_ref[...])
    if final_norm:
        y = _norm_rows(y, g_ref[...])
    y_ref[...] = y


def _combine(x, o, wts, g, final_norm, tm):
    T, D = x.shape
    nt = T // tm
    return pl.pallas_call(
        functools.partial(_combine_kernel, final_norm=final_norm),
        grid=(nt,),
        in_specs=[pl.BlockSpec((tm, D), lambda i: (i, 0)),
                  pl.BlockSpec((tm, D), lambda i: (i, 0)),
                  pl.BlockSpec((tm, D), lambda i: (i + nt, 0)),
                  pl.BlockSpec((tm, LANES), lambda i: (i, 0)),
                  pl.BlockSpec((1, D), lambda i: (0, 0))],
        out_specs=pl.BlockSpec((tm, D), lambda i: (i, 0)),
        out_shape=jax.ShapeDtypeStruct((T, D), F32),
        compiler_params=_params(("arbitrary",)),
        name="combine",
    )(x, o, o, wts, g.reshape(1, D).astype(F32))


PROJ_TM, PROJ_TN = 512, 1024
OUT_TM, OUT_TN = 512, 512
ROUTER_TM = 256
FFN_TM = 256
COMBINE_TM = 256


def kernel(x, norm_attn, w_in, rel_bias, out_norm_dil, out_norm_sb, w_out, norm_ffn,
           w_route_group, b_route_group, w_route_expert, b_route_expert,
           w_gate, w_up, w_down, norm_final):
    B, S, D = x.shape
    T = B * S
    depth = w_in.shape[0]
    n_tiles = 2 * T // FFN_TM + N_EXPERTS
    bias_tab = _dilated_bias_table(rel_bias)
    xt = x.reshape(T, D)
    hd, hs = N_HEADS_DIL, N_HEADS_SB

    for l in range(depth):
        proj = _norm_matmul([xt], [norm_attn[l]], w_in[l].astype(BF16), None, BF16,
                            PROJ_TM, PROJ_TN).reshape(B, S, -1)
        y_a = _dilated_attention(proj, bias_tab, 0, hd, 2 * hd, hd)
        y_b = _sb_attention(proj, 3 * hd, 3 * hd + hs, 3 * hd + 2 * hs, hs)
        xt = _norm_matmul([y_a.reshape(T, -1), y_b.reshape(T, -1)],
                          [out_norm_dil[l], out_norm_sb[l]], w_out[l].astype(BF16), xt, F32,
                          OUT_TM, OUT_TN)
        eid, wts, counts = _router(xt, norm_ffn[l], w_route_group[l], b_route_group[l],
                                   w_route_expert[l], b_route_expert[l], ROUTER_TM)
        plan = _dispatch_plan(eid, counts, FFN_TM, n_tiles)
        wgu = jnp.concatenate([w_gate[l], w_up[l]], axis=-1).astype(BF16)
        o = _moe_ffn(xt, norm_ffn[l], plan, wgu, w_down[l].astype(BF16), FFN_TM, n_tiles)
        xt = _combine(xt, o, wts, norm_final, l == depth - 1, COMBINE_TM)
    return xt.reshape(B, S, D)
```

```python
import functools
import math

import jax
import jax.numpy as jnp
from jax import lax
from jax.experimental import pallas as pl
from jax.experimental.pallas import tpu as pltpu

HEAD_DIM = 128
N_HEADS_DIL = 16
N_HEADS_SB = 16
DILATED_CONFIGS = ((128, 1), (512, 4), (2048, 16))
BLOCK = 128
N_BUCKETS = 32
MAX_DISTANCE = 2048
N_GROUPS = 4
EXPERTS_PER_GROUP = 8
N_EXPERTS = N_GROUPS * EXPERTS_PER_GROUP
D_EXPERT = 256
RMS_EPS = 1e-6
NEG_INF = -1e30

LOG2_E = math.log2(math.e)
LANES = 128
VMEM_LIMIT = 56 * 1024 * 1024

F32 = jnp.float32
BF16 = jnp.bfloat16


def _params(sem, vmem=VMEM_LIMIT):
    return pltpu.CompilerParams(dimension_semantics=sem, vmem_limit_bytes=vmem)


NORM_ROWS = 64


def _norm_rows(x, g):
    var = jnp.mean(x * x, axis=-1, keepdims=True)
    return (x * lax.rsqrt(var + RMS_EPS)) * g


def _norm_matmul_kernel(*refs, n_in, has_res):
    xs = refs[:n_in]
    gs = refs[n_in:2 * n_in]
    w_ref = refs[2 * n_in]
    pos = 2 * n_in + 1
    res_ref = refs[pos] if has_res else None
    pos += int(has_res)
    o_ref, h_ref = refs[pos], refs[pos + 1]

    @pl.when(pl.program_id(1) == 0)
    def _():
        tm = h_ref.shape[0]
        off = 0
        for x_ref, g_ref in zip(xs, gs):
            k = x_ref.shape[1]

            def body(r, _, x_ref=x_ref, g_ref=g_ref, off=off, k=k):
                rows = pl.ds(pl.multiple_of(r * NORM_ROWS, NORM_ROWS), NORM_ROWS)
                h = _norm_rows(x_ref[rows, :], g_ref[...])
                h_ref[rows, off:off + k] = h.astype(BF16)
                return 0

            lax.fori_loop(0, tm // NORM_ROWS, body, 0)
            off += k

    acc = jnp.dot(h_ref[...], w_ref[...], preferred_element_type=F32)
    if has_res:
        acc = res_ref[...] + acc
    o_ref[...] = acc.astype(o_ref.dtype)


def _norm_matmul(xs, gs, w, res, out_dtype, tm, tn):
    T = xs[0].shape[0]
    K, N = w.shape
    assert sum(x.shape[1] for x in xs) == K and T % tm == 0 and N % tn == 0
    n_in = len(xs)
    in_specs = [pl.BlockSpec((tm, x.shape[1]), lambda i, j: (i, 0)) for x in xs]
    in_specs += [pl.BlockSpec((1, x.shape[1]), lambda i, j: (0, 0)) for x in xs]
    in_specs += [pl.BlockSpec((K, tn), lambda i, j: (0, j))]
    args = list(xs) + [g.reshape(1, -1).astype(F32) for g in gs] + [w]
    if res is not None:
        in_specs.append(pl.BlockSpec((tm, tn), lambda i, j: (i, j)))
        args.append(res)
    return pl.pallas_call(
        functools.partial(_norm_matmul_kernel, n_in=n_in, has_res=res is not None),
        grid=(T // tm, N // tn),
        in_specs=in_specs,
        out_specs=pl.BlockSpec((tm, tn), lambda i, j: (i, j)),
        out_shape=jax.ShapeDtypeStruct((T, N), out_dtype),
        scratch_shapes=[pltpu.VMEM((tm, K), BF16)],
        compiler_params=_params(("arbitrary", "arbitrary")),
        name="norm_matmul",
    )(*args)


def _t5_bucket(dist):
    max_exact = N_BUCKETS // 2
    d_f = jnp.maximum(dist, 1).astype(F32)
    large = max_exact + (jnp.log(d_f / max_exact) / math.log(MAX_DISTANCE / max_exact)
                         * (N_BUCKETS - max_exact)).astype(jnp.int32)
    large = jnp.minimum(large, N_BUCKETS - 1)
    return jnp.where(dist < max_exact, dist, large)


def _dilated_bias_table(rel_bias):
    qi = jnp.arange(BLOCK)[:, None]
    kj = jnp.arange(2 * BLOCK)[None, :]
    dist = BLOCK + qi - kj
    tabs = []
    for window, dilation in DILATED_CONFIGS:
        band = (dist >= 0) & (dist <= window // dilation)
        bucket = _t5_bucket(jnp.maximum(dist, 0) * dilation)
        onehot = (bucket[:, :, None] == jnp.arange(N_BUCKETS)[None, None, :]).astype(F32)
        bias = jnp.einsum('qkb,bh->hqk', onehot, rel_bias.astype(F32),
                          precision=lax.Precision.HIGHEST)
        tabs.append(jnp.where(band[None], bias, NEG_INF))
    return jnp.stack(tabs, axis=0)


DIL_UNROLL = 8


def _strided_rows(start, dilation):
    if dilation == 1:
        return pl.ds(start, BLOCK)
    return pl.ds(start, BLOCK, stride=dilation)


def _dilated_kernel(q_ref, k_ref, v_ref, bias_ref, o_ref, q32, k32, v32, oc, lc):
    S = q_ref.shape[0]
    scale = HEAD_DIM ** -0.5
    q32[...] = q_ref[...].astype(F32)
    k32[...] = k_ref[...].astype(F32)
    v32[...] = v_ref[...].astype(F32)

    for ci, (_, dil) in enumerate(DILATED_CONFIGS):
        nb = S // dil // BLOCK

        def block(idx, ci=ci, dil=dil, nb=nb):
            g = idx // nb
            n = idx - g * nb
            start = g + dil * BLOCK * n
            pstart = g + dil * BLOCK * jnp.maximum(n - 1, 0)
            cur = _strided_rows(start, dil)
            prev = _strided_rows(pstart, dil)
            qs = q32[cur, :].astype(BF16)
            kc = jnp.concatenate([k32[prev, :], k32[cur, :]], axis=0).astype(BF16)
            vc = jnp.concatenate([v32[prev, :], v32[cur, :]], axis=0).astype(BF16)
            z = lax.dot_general(qs, kc, (((1,), (1,)), ((), ())),
                                preferred_element_type=F32) * scale
            bias = bias_ref[ci]
            col = lax.broadcasted_iota(jnp.int32, bias.shape, 1)
            valid = (bias > 0.5 * NEG_INF) & ((col >= BLOCK) | (n > 0))
            logits = jnp.where(valid, z + bias, NEG_INF)
            m = jnp.max(logits, axis=-1, keepdims=True)
            p = jnp.exp(logits - m)
            l = jnp.sum(p, axis=-1, keepdims=True)
            out = jnp.dot(p.astype(BF16), vc, preferred_element_type=F32) / l
            lse = m + jnp.log(l)
            oc[ci, cur, :] = out
            lc[ci, cur, :] = jnp.broadcast_to(lse, (BLOCK, HEAD_DIM))

        def body(t, _, block=block):
            for u in range(DIL_UNROLL):
                block(t * DIL_UNROLL + u)
            return 0

        lax.fori_loop(0, S // BLOCK // DIL_UNROLL, body, 0)

    def combine(r, _):
        rows = pl.ds(pl.multiple_of(r * BLOCK, BLOCK), BLOCK)
        l0, l1, l2 = lc[0, rows, :], lc[1, rows, :], lc[2, rows, :]
        m = jnp.maximum(jnp.maximum(l0, l1), l2)
        e0, e1, e2 = jnp.exp(l0 - m), jnp.exp(l1 - m), jnp.exp(l2 - m)
        tot = e0 + e1 + e2
        out = (e0 * oc[0, rows, :] + e1 * oc[1, rows, :] + e2 * oc[2, rows, :]) / tot
        o_ref[rows, :] = out.astype(o_ref.dtype)
        return 0

    lax.fori_loop(0, S // BLOCK, combine, 0)


def _dilated_attention(proj, bias_tab, col_q, col_k, col_v, n_heads):
    B, S, _ = proj.shape
    assert len(DILATED_CONFIGS) == 3
    assert all(S % (d * BLOCK) == 0 for _, d in DILATED_CONFIGS)
    assert (S // BLOCK) % DIL_UNROLL == 0

    def head_spec(col0):
        return pl.BlockSpec((None, S, HEAD_DIM), lambda b, h: (b, 0, col0 + h))

    n_cfg = len(DILATED_CONFIGS)
    return pl.pallas_call(
        _dilated_kernel,
        grid=(B, n_heads),
        in_specs=[head_spec(col_q), head_spec(col_k), head_spec(col_v),
                  pl.BlockSpec((n_cfg, None, BLOCK, 2 * BLOCK), lambda b, h: (0, h, 0, 0))],
        out_specs=pl.BlockSpec((None, S, HEAD_DIM), lambda b, h: (b, 0, h)),
        out_shape=jax.ShapeDtypeStruct((B, S, n_heads * HEAD_DIM), F32),
        scratch_shapes=[pltpu.VMEM((S, HEAD_DIM), F32)] * 3
        + [pltpu.VMEM((n_cfg, S, HEAD_DIM), F32)] * 2,
        compiler_params=_params(("arbitrary", "arbitrary")),
        name="dilated_attention",
    )(proj, proj, proj, bias_tab)


SB_TQ = 512
SB_KC = 256
SB_GROUP = SB_TQ // SB_KC


def _sb_kernel(q_ref, k_ref, v_ref, o_ref, tri_ref, acc_ref, carry_ref, lb_ref, tot_ref):
    S = q_ref.shape[0]
    scale = HEAD_DIM ** -0.5
    tri_ref[...] = jnp.where(lax.broadcasted_iota(jnp.int32, (SB_KC, SB_KC), 0)
                             > lax.broadcasted_iota(jnp.int32, (SB_KC, SB_KC), 1),
                             1.0, 0.0).astype(BF16)

    def stage_a(qbase, kstart, par, u, diag_off):
        q = q_ref[pl.ds(qbase, SB_TQ), :]
        z = lax.dot_general(q, k_ref[pl.ds(kstart, SB_KC), :], (((1,), (1,)), ((), ())),
                            preferred_element_type=F32) * (scale * LOG2_E)
        neg_abs = lax.bitcast_convert_type(
            lax.bitcast_convert_type(z, jnp.uint32) | jnp.uint32(0x80000000), F32)
        lg = jnp.log2(1.0 + jnp.exp2(neg_abs))
        log_beta = jnp.minimum(z, 0.0) - lg
        log_1m = log_beta - z
        if diag_off is not None:
            causal = (lax.broadcasted_iota(jnp.int32, (SB_TQ, SB_KC), 1) + diag_off
                      < lax.broadcasted_iota(jnp.int32, (SB_TQ, SB_KC), 0))
            log_1m = jnp.where(causal, log_1m, 0.0)
        rem = jnp.dot(log_1m.astype(BF16), tri_ref[...], preferred_element_type=F32)
        lb = log_beta + rem
        if diag_off is not None:
            lb = jnp.where(causal, lb, -jnp.inf)
        lb_ref[par, u] = lb
        tot = rem[:, 0:1] + log_1m[:, 0:1]
        tot_ref[par, u] = jnp.broadcast_to(tot, (SB_TQ, HEAD_DIM))

    def stage_b(kbase, par):
        carry = carry_ref[...]
        acc = acc_ref[...]
        for u in range(SB_GROUP):
            kstart = pl.multiple_of(kbase + SB_TQ - SB_KC * (u + 1), SB_KC)
            lb = lb_ref[par, u]
            a = jnp.concatenate([jnp.exp2(lb[:, c * HEAD_DIM:(c + 1) * HEAD_DIM] + carry)
                                 for c in range(SB_KC // HEAD_DIM)], axis=1)
            acc = acc + jnp.dot(a.astype(BF16), v_ref[pl.ds(kstart, SB_KC), :],
                                preferred_element_type=F32)
            carry = carry + tot_ref[par, u]
        carry_ref[...] = carry
        acc_ref[...] = acc

    def q_tile(i, _):
        qbase = pl.multiple_of(i * SB_TQ, SB_TQ)
        acc_ref[...] = jnp.zeros_like(acc_ref)
        carry_ref[...] = jnp.zeros_like(carry_ref)
        for u in range(SB_GROUP):
            off = SB_TQ - SB_KC * (u + 1)
            stage_a(qbase, pl.multiple_of(qbase + off, SB_KC), 0, u, off)

        def body(g, _):
            par = g % 2
            kbase = pl.multiple_of((i - g) * SB_TQ, SB_TQ)
            stage_b(kbase + SB_TQ, 1 - par)
            for u in range(SB_GROUP):
                stage_a(qbase, pl.multiple_of(kbase + SB_TQ - SB_KC * (u + 1), SB_KC), par, u, None)
            return 0

        lax.fori_loop(1, i + 1, body, 0)
        stage_b(0, i % 2)
        o_ref[pl.ds(qbase, SB_TQ), :] = acc_ref[...].astype(o_ref.dtype)
        return 0

    lax.fori_loop(0, S // SB_TQ, q_tile, 0)


def _sb_attention(proj, col_q, col_k, col_v, n_heads):
    B, S, _ = proj.shape
    assert S % SB_TQ == 0 and SB_TQ % SB_KC == 0 and SB_KC % HEAD_DIM == 0

    def head_spec(col0):
        return pl.BlockSpec((None, S, HEAD_DIM), lambda b, h: (b, 0, col0 + h))

    return pl.pallas_call(
        _sb_kernel,
        grid=(B, n_heads),
        in_specs=[head_spec(col_q), head_spec(col_k), head_spec(col_v)],
        out_specs=pl.BlockSpec((None, S, HEAD_DIM), lambda b, h: (b, 0, h)),
        out_shape=jax.ShapeDtypeStruct((B, S, n_heads * HEAD_DIM), F32),
        scratch_shapes=[pltpu.VMEM((SB_KC, SB_KC), BF16),
                        pltpu.VMEM((SB_TQ, HEAD_DIM), F32), pltpu.VMEM((SB_TQ, HEAD_DIM), F32),
                        pltpu.VMEM((2, SB_GROUP, SB_TQ, SB_KC), F32),
                        pltpu.VMEM((2, SB_GROUP, SB_TQ, HEAD_DIM), F32)],
        compiler_params=_params(("arbitrary", "arbitrary")),
        name="sb_attention",
    )(proj, proj, proj)


def _router_kernel(x_ref, g_ref, wr_ref, br_ref, id_ref, wt_ref, cnt_ref):
    h = _norm_rows(x_ref[...], g_ref[...])
    logits = jnp.dot(h, wr_ref[...], precision=lax.Precision.HIGHEST,
                     preferred_element_type=F32) + br_ref[...]
    lane = lax.broadcasted_iota(jnp.int32, logits.shape, 1).astype(F32)
    big = float(LANES)
    ninf = -jnp.inf

    def first_argmax(vals):
        top = jnp.max(vals, axis=-1, keepdims=True)
        idx = jnp.min(jnp.where(vals == top, lane, big), axis=-1, keepdims=True)
        return top, idx

    gl = jnp.where(lane < N_GROUPS, logits, ninf)
    gmax, g_top = first_argmax(gl)
    p_g = 1.0 / jnp.sum(jnp.exp(gl - gmax), axis=-1, keepdims=True)
    first = N_GROUPS + g_top * EXPERTS_PER_GROUP
    el = jnp.where((lane >= first) & (lane < first + EXPERTS_PER_GROUP), logits, ninf)
    v1, i1 = first_argmax(el)
    v2, i2 = first_argmax(jnp.where(lane == i1, ninf, el))
    t = jnp.exp(v2 - v1)
    w1 = p_g / (1.0 + t)
    w2 = p_g * t / (1.0 + t)
    ids = jnp.where(lane == 0, i1 - N_GROUPS, jnp.where(lane == 1, i2 - N_GROUPS, 0.0))
    id_ref[...] = ids.astype(jnp.int32)
    wt_ref[...] = jnp.where(lane == 0, w1, jnp.where(lane == 1, w2, 0.0))

    @pl.when(pl.program_id(0) == 0)
    def _():
        cnt_ref[...] = jnp.zeros_like(cnt_ref)

    chosen = jnp.where((lane == i1) | (lane == i2), 1.0, 0.0)
    cnt_ref[...] += jnp.sum(chosen, axis=0, keepdims=True)


def _router(x, g, w_rg, b_rg, w_re, b_re, tm):
    T, D = x.shape
    n_logits = N_GROUPS + N_EXPERTS
    wr = jnp.concatenate([w_rg, jnp.transpose(w_re, (1, 0, 2)).reshape(D, N_EXPERTS)], axis=1)
    wr = jnp.pad(wr.astype(F32), ((0, 0), (0, LANES - n_logits)))
    br = jnp.pad(jnp.concatenate([b_rg, b_re.reshape(-1)]).astype(F32), (0, LANES - n_logits))
    eid, wts, cnt = pl.pallas_call(
        _router_kernel,
        grid=(T // tm,),
        in_specs=[pl.BlockSpec((tm, D), lambda i: (i, 0)),
                  pl.BlockSpec((1, D), lambda i: (0, 0)),
                  pl.BlockSpec((D, LANES), lambda i: (0, 0)),
                  pl.BlockSpec((1, LANES), lambda i: (0, 0))],
        out_specs=[pl.BlockSpec((tm, LANES), lambda i: (i, 0))] * 2
        + [pl.BlockSpec((1, LANES), lambda i: (0, 0))],
        out_shape=[jax.ShapeDtypeStruct((T, LANES), jnp.int32),
                   jax.ShapeDtypeStruct((T, LANES), F32),
                   jax.ShapeDtypeStruct((1, LANES), F32)],
        compiler_params=_params(("arbitrary",)),
        name="router",
    )(x, g.reshape(1, D).astype(F32), wr, br.reshape(1, LANES))
    counts = cnt[0, N_GROUPS:N_GROUPS + N_EXPERTS].astype(jnp.int32)
    return eid[:, :2], wts, counts


def _dispatch_plan(eid, counts, tm, n_tiles):
    T = eid.shape[0]
    n_assign = 2 * T
    order = jnp.argsort(eid.T.reshape(-1), stable=True).astype(jnp.int32)
    tiles_per = (counts + tm - 1) // tm
    tile_end = jnp.cumsum(tiles_per)
    tile_idx = jnp.arange(n_tiles, dtype=jnp.int32)
    tile_expert = jnp.minimum(
        jnp.sum((tile_end[None, :] <= tile_idx[:, None]).astype(jnp.int32), axis=1), N_EXPERTS - 1)
    onehot = (tile_expert[:, None] == jnp.arange(N_EXPERTS, dtype=jnp.int32)[None, :])

    def of_tile(per_expert):
        return jnp.sum(jnp.where(onehot, per_expert[None, :], 0), axis=1)

    local = (tile_idx - of_tile(tile_end - tiles_per)) * tm
    tile_nvalid = jnp.where(tile_idx < tile_end[-1], jnp.clip(of_tile(counts) - local, 0, tm), 0)
    tile_base = jnp.clip(of_tile(jnp.cumsum(counts) - counts) + local, 0, n_assign)
    order_ext = jnp.concatenate([order, jnp.zeros((tm,), jnp.int32)])
    ids = jax.vmap(lambda b: lax.dynamic_slice(order_ext, (b,), (tm,)))(tile_base)
    row_src = jnp.where(ids >= T, ids - T, ids)
    return tile_expert, tile_nvalid.astype(jnp.int32), row_src, ids


DMA_ISSUE_UNROLL = 8


def _for_rows(n, fn):
    n_groups = n // DMA_ISSUE_UNROLL

    def group(gi, _):
        for u in range(DMA_ISSUE_UNROLL):
            fn(gi * DMA_ISSUE_UNROLL + u)
        return 0

    def single(r, _):
        fn(r)
        return 0

    lax.fori_loop(0, n_groups, group, 0)
    lax.fori_loop(n_groups * DMA_ISSUE_UNROLL, n, single, 0)


def _wait_rows(n, max_rows, copy_of_rows):
    k = max_rows
    while k >= 1:
        @pl.when((n & k) != 0)
        def _(k=k):
            copy_of_rows(k).wait()
        k //= 2


def _ffn_kernel(texp_ref, nval_ref, src_ref, src_next_ref, dst_ref, g_ref, wg_ref, wu_ref, wd_ref,
                x_hbm, o_hbm, xbuf, hbuf, obuf, wgu_bf, wd_bf, gsem, ssem):
    i = pl.program_id(0)
    n_tiles = pl.num_programs(0)
    tm = xbuf.shape[1]
    f = wd_ref.shape[0]
    slot = i % 2
    other = 1 - slot
    nval = nval_ref[i]
    nval_next = jnp.where(i + 1 < n_tiles, nval_ref[jnp.minimum(i + 1, n_tiles - 1)], 0)

    def start_gather(idx_ref, s, n):
        _for_rows(n, lambda r: pltpu.make_async_copy(
            x_hbm.at[pl.ds(idx_ref[0, 0, r], 1)], xbuf.at[s, pl.ds(r, 1)], gsem.at[s]).start())

    def wait_gather(s, n):
        _wait_rows(n, tm, lambda k: pltpu.make_async_copy(
            x_hbm.at[pl.ds(0, k)], xbuf.at[s, pl.ds(0, k)], gsem.at[s]))

    def wait_scatter(s, n):
        _wait_rows(n, tm, lambda k: pltpu.make_async_copy(
            obuf.at[s, pl.ds(0, k)], o_hbm.at[pl.ds(0, k)], ssem.at[s]))

    @pl.when(i == 0)
    def _():
        xbuf[...] = jnp.zeros_like(xbuf)
        start_gather(src_ref, 0, nval)

    @pl.when(nval > 0)
    def _():
        start_gather(src_next_ref, other, nval_next)
        wait_gather(slot, nval)

        @pl.when(i >= 2)
        def _():
            wait_scatter(slot, nval_ref[jnp.maximum(i - 2, 0)])

        def norm(r, _):
            rows = pl.ds(pl.multiple_of(r * NORM_ROWS, NORM_ROWS), NORM_ROWS)
            hbuf[rows, :] = _norm_rows(xbuf[slot, rows, :], g_ref[...]).astype(BF16)
            return 0

        lax.fori_loop(0, tm // NORM_ROWS, norm, 0)

        @pl.when((i == 0) | (texp_ref[i] != texp_ref[jnp.maximum(i - 1, 0)]))
        def _():
            wgu_bf[:, :f] = wg_ref[...].astype(BF16)
            wgu_bf[:, f:] = wu_ref[...].astype(BF16)
            wd_bf[...] = wd_ref[...].astype(BF16)

        h = jnp.dot(hbuf[...], wgu_bf[...], preferred_element_type=F32)
        hg, hu = h[:, :f], h[:, f:]
        hid = (hg * (1.0 / (1.0 + jnp.exp(-hg)))) * hu
        obuf[slot] = jnp.dot(hid.astype(BF16), wd_bf[...], preferred_element_type=F32)
        _for_rows(nval, lambda r: pltpu.make_async_copy(
            obuf.at[slot, pl.ds(r, 1)], o_hbm.at[pl.ds(dst_ref[0, 0, r], 1)],
            ssem.at[slot]).start())

        @pl.when(nval_next == 0)
        def _():
            @pl.when(i >= 1)
            def _():
                wait_scatter(other, nval_ref[jnp.maximum(i - 1, 0)])

            wait_scatter(slot, nval)


def _moe_ffn(x, g, plan, wg, wu, wd, tm, n_tiles):
    T, D = x.shape
    tile_expert, tile_nvalid, row_src, row_dst = plan
    f = wg.shape[2]
    row_src = row_src.reshape(n_tiles, 1, tm)

    def idx_spec(step):
        return pl.BlockSpec((1, 1, tm),
                            lambda i, te, nv: (jnp.minimum(i + step, n_tiles - 1), 0, 0),
                            memory_space=pltpu.SMEM)

    grid_spec = pltpu.PrefetchScalarGridSpec(
        num_scalar_prefetch=2,
        grid=(n_tiles,),
        in_specs=[
            idx_spec(0), idx_spec(1), idx_spec(0),
            pl.BlockSpec((1, D), lambda i, te, nv: (0, 0)),
            pl.BlockSpec((None, D, f), lambda i, te, nv: (te[i], 0, 0)),
            pl.BlockSpec((None, D, f), lambda i, te, nv: (te[i], 0, 0)),
            pl.BlockSpec((None, f, D), lambda i, te, nv: (te[i], 0, 0)),
            pl.BlockSpec(memory_space=pl.ANY),
        ],
        out_specs=pl.BlockSpec(memory_space=pl.ANY),
        scratch_shapes=[pltpu.VMEM((2, tm, D), F32), pltpu.VMEM((tm, D), BF16),
                        pltpu.VMEM((2, tm, D), F32),
                        pltpu.VMEM((D, 2 * f), BF16), pltpu.VMEM((f, D), BF16),
                        pltpu.SemaphoreType.DMA((2,)), pltpu.SemaphoreType.DMA((2,))],
    )
    return pl.pallas_call(
        _ffn_kernel,
        grid_spec=grid_spec,
        out_shape=jax.ShapeDtypeStruct((2 * T, D), F32),
        compiler_params=_params(("arbitrary",)),
        name="moe_ffn",
    )(tile_expert, tile_nvalid, row_src, row_src, row_dst.reshape(n_tiles, 1, tm),
      g.reshape(1, D).astype(F32), wg, wu, wd, x)


def _combine_kernel(x_ref, o0_ref, o1_ref, wt_ref, g_ref, y_ref, *, final_norm):
    wt = wt_ref[...]
    y = x_ref[...] + (wt[:, 0:1] * o0_ref[...] + wt[:, 1:2] * o1_ref[...])
    if final_norm:
        y = _norm_rows(y, g_ref[...])
    y_ref[...] = y


def _combine(x, o, wts, g, final_norm, tm):
    T, D = x.shape
    nt = T // tm
    return pl.pallas_call(
        functools.partial(_combine_kernel, final_norm=final_norm),
        grid=(nt,),
        in_specs=[pl.BlockSpec((tm, D), lambda i: (i, 0)),
                  pl.BlockSpec((tm, D), lambda i: (i, 0)),
                  pl.BlockSpec((tm, D), lambda i: (i + nt, 0)),
                  pl.BlockSpec((tm, LANES), lambda i: (i, 0)),
                  pl.BlockSpec((1, D), lambda i: (0, 0))],
        out_specs=pl.BlockSpec((tm, D), lambda i: (i, 0)),
        out_shape=jax.ShapeDtypeStruct((T, D), F32),
        compiler_params=_params(("arbitrary",)),
        name="combine",
    )(x, o, o, wts, g.reshape(1, D).astype(F32))


PROJ_TM, PROJ_TN = 512, 1024
OUT_TM, OUT_TN = 512, 1024
ROUTER_TM = 256
FFN_TM = 256
COMBINE_TM = 256


def kernel(x, norm_attn, w_in, rel_bias, out_norm_dil, out_norm_sb, w_out, norm_ffn,
           w_route_group, b_route_group, w_route_expert, b_route_expert,
           w_gate, w_up, w_down, norm_final):
    B, S, D = x.shape
    T = B * S
    depth = w_in.shape[0]
    n_tiles = 2 * T // FFN_TM + N_EXPERTS
    bias_tab = _dilated_bias_table(rel_bias)
    xt = x.reshape(T, D)
    hd, hs = N_HEADS_DIL, N_HEADS_SB

    for l in range(depth):
        proj = _norm_matmul([xt], [norm_attn[l]], w_in[l].astype(BF16), None, BF16,
                            PROJ_TM, PROJ_TN).reshape(B, S, -1)
        y_a = _dilated_attention(proj, bias_tab, 0, hd, 2 * hd, hd)
        y_b = _sb_attention(proj, 3 * hd, 3 * hd + hs, 3 * hd + 2 * hs, hs)
        xt = _norm_matmul([y_a.reshape(T, -1), y_b.reshape(T, -1)],
                          [out_norm_dil[l], out_norm_sb[l]], w_out[l].astype(BF16), xt, F32,
                          OUT_TM, OUT_TN)
        eid, wts, counts = _router(xt, norm_ffn[l], w_route_group[l], b_route_group[l],
                                   w_route_expert[l], b_route_expert[l], ROUTER_TM)
        plan = _dispatch_plan(eid, counts, FFN_TM, n_tiles)
        o = _moe_ffn(xt, norm_ffn[l], plan, w_gate[l], w_up[l], w_down[l], FFN_TM, n_tiles)
        xt = _combine(xt, o, wts, norm_final, l == depth - 1, COMBINE_TM)
    return xt.reshape(B, S, D)
```

```python
import functools
import math

import jax
import jax.numpy as jnp
from jax import lax
from jax.experimental import pallas as pl
from jax.experimental.pallas import tpu as pltpu

HEAD_DIM = 128
N_HEADS_DIL = 16
N_HEADS_SB = 16
DILATED_CONFIGS = ((128, 1), (512, 4), (2048, 16))
BLOCK = 128
N_BUCKETS = 32
MAX_DISTANCE = 2048
N_GROUPS = 4
EXPERTS_PER_GROUP = 8
N_EXPERTS = N_GROUPS * EXPERTS_PER_GROUP
D_EXPERT = 256
RMS_EPS = 1e-6
NEG_INF = -1e30

LOG2_E = math.log2(math.e)
LANES = 128
VMEM_LIMIT = 56 * 1024 * 1024

F32 = jnp.float32
BF16 = jnp.bfloat16


def _params(sem, vmem=VMEM_LIMIT):
    return pltpu.CompilerParams(dimension_semantics=sem, vmem_limit_bytes=vmem)


NORM_ROWS = 64


def _norm_rows(x, g):
    var = jnp.mean(x * x, axis=-1, keepdims=True)
    return (x * lax.rsqrt(var + RMS_EPS)) * g


def _norm_matmul_kernel(*refs, n_in, has_res):
    xs = refs[:n_in]
    gs = refs[n_in:2 * n_in]
    w_ref = refs[2 * n_in]
    pos = 2 * n_in + 1
    res_ref = refs[pos] if has_res else None
    pos += int(has_res)
    o_ref, h_ref = refs[pos], refs[pos + 1]

    @pl.when(pl.program_id(1) == 0)
    def _():
        tm = h_ref.shape[0]
        off = 0
        for x_ref, g_ref in zip(xs, gs):
            k = x_ref.shape[1]

            def body(r, _, x_ref=x_ref, g_ref=g_ref, off=off, k=k):
                rows = pl.ds(pl.multiple_of(r * NORM_ROWS, NORM_ROWS), NORM_ROWS)
                h = _norm_rows(x_ref[rows, :], g_ref[...])
                h_ref[rows, off:off + k] = h.astype(BF16)
                return 0

            lax.fori_loop(0, tm // NORM_ROWS, body, 0)
            off += k

    acc = jnp.dot(h_ref[...], w_ref[...], preferred_element_type=F32)
    if has_res:
        acc = res_ref[...] + acc
    o_ref[...] = acc.astype(o_ref.dtype)


def _norm_matmul(xs, gs, w, layer, res, out_dtype, tm, tn):
    T = xs[0].shape[0]
    _, K, N = w.shape
    assert sum(x.shape[1] for x in xs) == K and T % tm == 0 and N % tn == 0
    n_in = len(xs)
    in_specs = [pl.BlockSpec((tm, x.shape[1]), lambda i, j: (i, 0)) for x in xs]
    in_specs += [pl.BlockSpec((1, x.shape[1]), lambda i, j: (0, 0)) for x in xs]
    in_specs += [pl.BlockSpec((None, K, tn), lambda i, j: (layer, 0, j))]
    args = list(xs) + [g.reshape(1, -1).astype(F32) for g in gs] + [w]
    if res is not None:
        in_specs.append(pl.BlockSpec((tm, tn), lambda i, j: (i, j)))
        args.append(res)
    return pl.pallas_call(
        functools.partial(_norm_matmul_kernel, n_in=n_in, has_res=res is not None),
        grid=(T // tm, N // tn),
        in_specs=in_specs,
        out_specs=pl.BlockSpec((tm, tn), lambda i, j: (i, j)),
        out_shape=jax.ShapeDtypeStruct((T, N), out_dtype),
        scratch_shapes=[pltpu.VMEM((tm, K), BF16)],
        compiler_params=_params(("arbitrary", "arbitrary")),
        name="norm_matmul",
    )(*args)


def _t5_bucket(dist):
    max_exact = N_BUCKETS // 2
    d_f = jnp.maximum(dist, 1).astype(F32)
    large = max_exact + (jnp.log(d_f / max_exact) / math.log(MAX_DISTANCE / max_exact)
                         * (N_BUCKETS - max_exact)).astype(jnp.int32)
    large = jnp.minimum(large, N_BUCKETS - 1)
    return jnp.where(dist < max_exact, dist, large)


def _dilated_bias_table(rel_bias):
    qi = jnp.arange(BLOCK)[:, None]
    kj = jnp.arange(2 * BLOCK)[None, :]
    dist = BLOCK + qi - kj
    tabs = []
    for window, dilation in DILATED_CONFIGS:
        band = (dist >= 0) & (dist <= window // dilation)
        bucket = _t5_bucket(jnp.maximum(dist, 0) * dilation)
        onehot = (bucket[:, :, None] == jnp.arange(N_BUCKETS)[None, None, :]).astype(F32)
        bias = jnp.einsum('qkb,bh->hqk', onehot, rel_bias.astype(F32),
                          precision=lax.Precision.HIGHEST)
        tabs.append(jnp.where(band[None], bias, NEG_INF))
    return jnp.stack(tabs, axis=0)


DIL_UNROLL = 8


def _strided_rows(start, dilation):
    if dilation == 1:
        return pl.ds(start, BLOCK)
    return pl.ds(start, BLOCK, stride=dilation)


def _dilated_kernel(q_ref, k_ref, v_ref, bias_ref, o_ref, q32, k32, v32, oc, lc):
    S = q_ref.shape[0]
    scale = HEAD_DIM ** -0.5
    q32[...] = q_ref[...].astype(F32)
    k32[...] = k_ref[...].astype(F32)
    v32[...] = v_ref[...].astype(F32)

    for ci, (_, dil) in enumerate(DILATED_CONFIGS):
        nb = S // dil // BLOCK

        def block(idx, ci=ci, dil=dil, nb=nb):
            g = idx // nb
            n = idx - g * nb
            start = g + dil * BLOCK * n
            pstart = g + dil * BLOCK * jnp.maximum(n - 1, 0)
            cur = _strided_rows(start, dil)
            prev = _strided_rows(pstart, dil)
            qs = q32[cur, :].astype(BF16)
            kc = jnp.concatenate([k32[prev, :], k32[cur, :]], axis=0).astype(BF16)
            vc = jnp.concatenate([v32[prev, :], v32[cur, :]], axis=0).astype(BF16)
            z = lax.dot_general(qs, kc, (((1,), (1,)), ((), ())),
                                preferred_element_type=F32) * scale
            bias = bias_ref[ci]
            col = lax.broadcasted_iota(jnp.int32, bias.shape, 1)
            valid = (bias > 0.5 * NEG_INF) & ((col >= BLOCK) | (n > 0))
            logits = jnp.where(valid, z + bias, NEG_INF)
            m = jnp.max(logits, axis=-1, keepdims=True)
            p = jnp.exp(logits - m)
            l = jnp.sum(p, axis=-1, keepdims=True)
            out = jnp.dot(p.astype(BF16), vc, preferred_element_type=F32) / l
            lse = m + jnp.log(l)
            oc[ci, cur, :] = out
            lc[ci, cur, :] = jnp.broadcast_to(lse, (BLOCK, HEAD_DIM))

        def body(t, _, block=block):
            for u in range(DIL_UNROLL):
                block(t * DIL_UNROLL + u)
            return 0

        lax.fori_loop(0, S // BLOCK // DIL_UNROLL, body, 0)

    def combine(r, _):
        rows = pl.ds(pl.multiple_of(r * BLOCK, BLOCK), BLOCK)
        l0, l1, l2 = lc[0, rows, :], lc[1, rows, :], lc[2, rows, :]
        m = jnp.maximum(jnp.maximum(l0, l1), l2)
        e0, e1, e2 = jnp.exp(l0 - m), jnp.exp(l1 - m), jnp.exp(l2 - m)
        tot = e0 + e1 + e2
        out = (e0 * oc[0, rows, :] + e1 * oc[1, rows, :] + e2 * oc[2, rows, :]) / tot
        o_ref[rows, :] = out.astype(o_ref.dtype)
        return 0

    lax.fori_loop(0, S // BLOCK, combine, 0)


def _dilated_attention(proj, bias_tab, col_q, col_k, col_v, n_heads):
    B, S, _ = proj.shape
    assert len(DILATED_CONFIGS) == 3
    assert all(S % (d * BLOCK) == 0 for _, d in DILATED_CONFIGS)
    assert (S // BLOCK) % DIL_UNROLL == 0

    def head_spec(col0):
        return pl.BlockSpec((None, S, HEAD_DIM), lambda b, h: (b, 0, col0 + h))

    n_cfg = len(DILATED_CONFIGS)
    return pl.pallas_call(
        _dilated_kernel,
        grid=(B, n_heads),
        in_specs=[head_spec(col_q), head_spec(col_k), head_spec(col_v),
                  pl.BlockSpec((n_cfg, None, BLOCK, 2 * BLOCK), lambda b, h: (0, h, 0, 0))],
        out_specs=pl.BlockSpec((None, S, HEAD_DIM), lambda b, h: (b, 0, h)),
        out_shape=jax.ShapeDtypeStruct((B, S, n_heads * HEAD_DIM), F32),
        scratch_shapes=[pltpu.VMEM((S, HEAD_DIM), F32)] * 3
        + [pltpu.VMEM((n_cfg, S, HEAD_DIM), F32)] * 2,
        compiler_params=_params(("arbitrary", "arbitrary")),
        name="dilated_attention",
    )(proj, proj, proj, bias_tab)


SB_TQ = 512
SB_KC = 256
SB_GROUP = SB_TQ // SB_KC


def _sb_kernel(q_ref, k_ref, v_ref, o_ref, tri_ref, acc_ref, carry_ref,
               l1m_ref, lbeta_ref, lb_ref, tot_ref):
    S = q_ref.shape[0]
    n_tiles = S // SB_TQ
    scale = HEAD_DIM ** -0.5
    tri_ref[...] = jnp.where(lax.broadcasted_iota(jnp.int32, (SB_KC, SB_KC), 0)
                             > lax.broadcasted_iota(jnp.int32, (SB_KC, SB_KC), 1),
                             1.0, 0.0).astype(BF16)

    def chunk_start(kbase, u):
        return pl.multiple_of(kbase + SB_TQ - SB_KC * (u + 1), SB_KC)

    def scores(qbase, kbase, par, diag):
        q = q_ref[pl.ds(qbase, SB_TQ), :]
        for u in range(SB_GROUP):
            z = lax.dot_general(q, k_ref[pl.ds(chunk_start(kbase, u), SB_KC), :],
                                (((1,), (1,)), ((), ())),
                                preferred_element_type=F32) * (scale * LOG2_E)
            neg_abs = lax.bitcast_convert_type(
                lax.bitcast_convert_type(z, jnp.uint32) | jnp.uint32(0x80000000), F32)
            lg = jnp.log(1.0 + jnp.exp2(neg_abs)) * LOG2_E
            log_beta = jnp.minimum(z, 0.0) - lg
            log_1m = log_beta - z
            if diag:
                off = SB_TQ - SB_KC * (u + 1)
                causal = (lax.broadcasted_iota(jnp.int32, (SB_TQ, SB_KC), 1) + off
                          < lax.broadcasted_iota(jnp.int32, (SB_TQ, SB_KC), 0))
                log_1m = jnp.where(causal, log_1m, 0.0)
                log_beta = jnp.where(causal, log_beta, -jnp.inf)
            l1m_ref[par, u] = log_1m.astype(BF16)
            lbeta_ref[par, u] = log_beta

    def sums(par):
        for u in range(SB_GROUP):
            l1m = l1m_ref[par, u]
            rem = jnp.dot(l1m, tri_ref[...], preferred_element_type=F32)
            lb_ref[par, u] = lbeta_ref[par, u] + rem
            tot = rem[:, 0:1] + l1m[:, 0:1].astype(F32)
            tot_ref[par, u] = jnp.broadcast_to(tot, (SB_TQ, HEAD_DIM))

    def values(kbase, par):
        carry = carry_ref[...]
        acc = acc_ref[...]
        for u in range(SB_GROUP):
            lb = lb_ref[par, u]
            a = jnp.concatenate([jnp.exp2(lb[:, c * HEAD_DIM:(c + 1) * HEAD_DIM] + carry)
                                 for c in range(SB_KC // HEAD_DIM)], axis=1)
            acc = acc + jnp.dot(a.astype(BF16), v_ref[pl.ds(chunk_start(kbase, u), SB_KC), :],
                                preferred_element_type=F32)
            carry = carry + tot_ref[par, u]
        carry_ref[...] = carry
        acc_ref[...] = acc

    def start_tile(qbase):
        acc_ref[...] = jnp.zeros_like(acc_ref)
        carry_ref[...] = jnp.zeros_like(carry_ref)
        scores(qbase, qbase, 0, True)
        sums(0)

    start_tile(0)
    values(0, 0)
    o_ref[pl.ds(0, SB_TQ), :] = acc_ref[...].astype(o_ref.dtype)

    def q_tile(i, _):
        qbase = pl.multiple_of(i * SB_TQ, SB_TQ)

        def kbase_of(g):
            return pl.multiple_of((i - g) * SB_TQ, SB_TQ)

        start_tile(qbase)
        scores(qbase, kbase_of(1), 1, False)

        def step(s, par):
            values(kbase_of(s - 2), par)
            sums(1 - par)
            scores(qbase, kbase_of(s), par, False)

        def two_steps(t, _):
            step(2 + 2 * t, 0)
            step(3 + 2 * t, 1)
            return 0

        n_steps = i - 1
        lax.fori_loop(0, n_steps // 2, two_steps, 0)

        @pl.when(n_steps % 2 == 1)
        def _():
            step(i, 0)
        last = i % 2
        values(kbase_of(i - 1), 1 - last)
        sums(last)
        values(kbase_of(i), last)
        o_ref[pl.ds(qbase, SB_TQ), :] = acc_ref[...].astype(o_ref.dtype)
        return 0

    lax.fori_loop(1, n_tiles, q_tile, 0)


def _sb_attention(proj, col_q, col_k, col_v, n_heads):
    B, S, _ = proj.shape
    assert S % SB_TQ == 0 and SB_TQ % SB_KC == 0 and SB_KC % HEAD_DIM == 0

    def head_spec(col0):
        return pl.BlockSpec((None, S, HEAD_DIM), lambda b, h: (b, 0, col0 + h))

    stage = (2, SB_GROUP, SB_TQ, SB_KC)
    return pl.pallas_call(
        _sb_kernel,
        grid=(B, n_heads),
        in_specs=[head_spec(col_q), head_spec(col_k), head_spec(col_v)],
        out_specs=pl.BlockSpec((None, S, HEAD_DIM), lambda b, h: (b, 0, h)),
        out_shape=jax.ShapeDtypeStruct((B, S, n_heads * HEAD_DIM), F32),
        scratch_shapes=[pltpu.VMEM((SB_KC, SB_KC), BF16),
                        pltpu.VMEM((SB_TQ, HEAD_DIM), F32), pltpu.VMEM((SB_TQ, HEAD_DIM), F32),
                        pltpu.VMEM(stage, BF16), pltpu.VMEM(stage, F32), pltpu.VMEM(stage, F32),
                        pltpu.VMEM((2, SB_GROUP, SB_TQ, HEAD_DIM), F32)],
        compiler_params=_params(("arbitrary", "arbitrary")),
        name="sb_attention",
    )(proj, proj, proj)


def _router_kernel(x_ref, g_ref, wr_ref, br_ref, id_ref, wt_ref, cnt_ref):
    h = _norm_rows(x_ref[...], g_ref[...])
    logits = jnp.dot(h, wr_ref[...], precision=lax.Precision.HIGHEST,
                     preferred_element_type=F32) + br_ref[...]
    lane = lax.broadcasted_iota(jnp.int32, logits.shape, 1).astype(F32)
    big = float(LANES)
    ninf = -jnp.inf

    def first_argmax(vals):
        top = jnp.max(vals, axis=-1, keepdims=True)
        idx = jnp.min(jnp.where(vals == top, lane, big), axis=-1, keepdims=True)
        return top, idx

    gl = jnp.where(lane < N_GROUPS, logits, ninf)
    gmax, g_top = first_argmax(gl)
    p_g = 1.0 / jnp.sum(jnp.exp(gl - gmax), axis=-1, keepdims=True)
    first = N_GROUPS + g_top * EXPERTS_PER_GROUP
    el = jnp.where((lane >= first) & (lane < first + EXPERTS_PER_GROUP), logits, ninf)
    v1, i1 = first_argmax(el)
    v2, i2 = first_argmax(jnp.where(lane == i1, ninf, el))
    t = jnp.exp(v2 - v1)
    w1 = p_g / (1.0 + t)
    w2 = p_g * t / (1.0 + t)
    ids = jnp.where(lane == 0, i1 - N_GROUPS, jnp.where(lane == 1, i2 - N_GROUPS, 0.0))
    id_ref[...] = ids.astype(jnp.int32)
    wt_ref[...] = jnp.where(lane == 0, w1, jnp.where(lane == 1, w2, 0.0))

    @pl.when(pl.program_id(0) == 0)
    def _():
        cnt_ref[...] = jnp.zeros_like(cnt_ref)

    chosen = jnp.where((lane == i1) | (lane == i2), 1.0, 0.0)
    cnt_ref[...] += jnp.sum(chosen, axis=0, keepdims=True)


def _router(x, g, w_rg, b_rg, w_re, b_re, tm):
    T, D = x.shape
    n_logits = N_GROUPS + N_EXPERTS
    wr = jnp.concatenate([w_rg, jnp.transpose(w_re, (1, 0, 2)).reshape(D, N_EXPERTS)], axis=1)
    wr = jnp.pad(wr.astype(F32), ((0, 0), (0, LANES - n_logits)))
    br = jnp.pad(jnp.concatenate([b_rg, b_re.reshape(-1)]).astype(F32), (0, LANES - n_logits))
    eid, wts, cnt = pl.pallas_call(
        _router_kernel,
        grid=(T // tm,),
        in_specs=[pl.BlockSpec((tm, D), lambda i: (i, 0)),
                  pl.BlockSpec((1, D), lambda i: (0, 0)),
                  pl.BlockSpec((D, LANES), lambda i: (0, 0)),
                  pl.BlockSpec((1, LANES), lambda i: (0, 0))],
        out_specs=[pl.BlockSpec((tm, LANES), lambda i: (i, 0))] * 2
        + [pl.BlockSpec((1, LANES), lambda i: (0, 0))],
        out_shape=[jax.ShapeDtypeStruct((T, LANES), jnp.int32),
                   jax.ShapeDtypeStruct((T, LANES), F32),
                   jax.ShapeDtypeStruct((1, LANES), F32)],
        compiler_params=_params(("arbitrary",)),
        name="router",
    )(x, g.reshape(1, D).astype(F32), wr, br.reshape(1, LANES))
    counts = cnt[0, N_GROUPS:N_GROUPS + N_EXPERTS].astype(jnp.int32)
    return eid[:, :2], wts, counts


def _dispatch_plan(eid, counts, tm, n_tiles):
    T = eid.shape[0]
    n_assign = 2 * T
    order = jnp.argsort(eid.T.reshape(-1), stable=True).astype(jnp.int32)
    tiles_per = (counts + tm - 1) // tm
    tile_end = jnp.cumsum(tiles_per)
    tile_idx = jnp.arange(n_tiles, dtype=jnp.int32)
    tile_expert = jnp.minimum(
        jnp.sum((tile_end[None, :] <= tile_idx[:, None]).astype(jnp.int32), axis=1), N_EXPERTS - 1)
    onehot = (tile_expert[:, None] == jnp.arange(N_EXPERTS, dtype=jnp.int32)[None, :])

    def of_tile(per_expert):
        return jnp.sum(jnp.where(onehot, per_expert[None, :], 0), axis=1)

    local = (tile_idx - of_tile(tile_end - tiles_per)) * tm
    tile_nvalid = jnp.where(tile_idx < tile_end[-1], jnp.clip(of_tile(counts) - local, 0, tm), 0)

    seg_start = jnp.cumsum(counts) - counts
    row_start = (tile_end - tiles_per) * tm
    order_ext = jnp.concatenate([order, jnp.zeros((n_assign,), jnp.int32)])

    def place(e, rows):
        seg = lax.dynamic_slice(order_ext, (seg_start[e],), (n_assign,))
        return lax.dynamic_update_slice(rows, seg, (row_start[e],))

    rows = lax.fori_loop(0, N_EXPERTS, place, jnp.zeros((n_tiles * tm + n_assign,), jnp.int32))
    ids = rows[:n_tiles * tm].reshape(n_tiles, tm)
    row_src = jnp.where(ids >= T, ids - T, ids)
    return tile_expert, tile_nvalid.astype(jnp.int32), row_src, ids


DMA_ISSUE_UNROLL = 8


def _for_rows(n, fn):
    n_groups = n // DMA_ISSUE_UNROLL

    def group(gi, _):
        for u in range(DMA_ISSUE_UNROLL):
            fn(gi * DMA_ISSUE_UNROLL + u)
        return 0

    def single(r, _):
        fn(r)
        return 0

    lax.fori_loop(0, n_groups, group, 0)
    lax.fori_loop(n_groups * DMA_ISSUE_UNROLL, n, single, 0)


def _wait_rows(n, max_rows, copy_of_rows):
    k = max_rows
    while k >= 1:
        @pl.when((n & k) != 0)
        def _(k=k):
            copy_of_rows(k).wait()
        k //= 2


def _ffn_kernel(texp_ref, nval_ref, src_ref, src_next_ref, dst_ref, g_ref, wg_ref, wu_ref, wd_ref,
                x_hbm, o_hbm, xbuf, hbuf, obuf, wgu_bf, wd_bf, gsem, ssem):
    i = pl.program_id(0)
    n_tiles = pl.num_programs(0)
    tm = xbuf.shape[1]
    f = wd_ref.shape[0]
    slot = i % 2
    other = 1 - slot
    nval = nval_ref[i]
    nval_next = jnp.where(i + 1 < n_tiles, nval_ref[jnp.minimum(i + 1, n_tiles - 1)], 0)

    def start_gather(idx_ref, s, n):
        _for_rows(n, lambda r: pltpu.make_async_copy(
            x_hbm.at[pl.ds(idx_ref[0, 0, r], 1)], xbuf.at[s, pl.ds(r, 1)], gsem.at[s]).start())

    def wait_gather(s, n):
        _wait_rows(n, tm, lambda k: pltpu.make_async_copy(
            x_hbm.at[pl.ds(0, k)], xbuf.at[s, pl.ds(0, k)], gsem.at[s]))

    def wait_scatter(s, n):
        _wait_rows(n, tm, lambda k: pltpu.make_async_copy(
            obuf.at[s, pl.ds(0, k)], o_hbm.at[pl.ds(0, k)], ssem.at[s]))

    @pl.when(i == 0)
    def _():
        xbuf[...] = jnp.zeros_like(xbuf)
        start_gather(src_ref, 0, nval)

    @pl.when(nval > 0)
    def _():
        start_gather(src_next_ref, other, nval_next)
        wait_gather(slot, nval)

        @pl.when(i >= 2)
        def _():
            wait_scatter(slot, nval_ref[jnp.maximum(i - 2, 0)])

        def norm(r, _):
            rows = pl.ds(pl.multiple_of(r * NORM_ROWS, NORM_ROWS), NORM_ROWS)
            hbuf[rows, :] = _norm_rows(xbuf[slot, rows, :], g_ref[...]).astype(BF16)
            return 0

        lax.fori_loop(0, tm // NORM_ROWS, norm, 0)

        @pl.when((i == 0) | (texp_ref[i] != texp_ref[jnp.maximum(i - 1, 0)]))
        def _():
            wgu_bf[:, :f] = wg_ref[...].astype(BF16)
            wgu_bf[:, f:] = wu_ref[...].astype(BF16)
            wd_bf[...] = wd_ref[...].astype(BF16)

        h = jnp.dot(hbuf[...], wgu_bf[...], preferred_element_type=F32)
        hg, hu = h[:, :f], h[:, f:]
        hid = (hg * (1.0 / (1.0 + jnp.exp(-hg)))) * hu
        obuf[slot] = jnp.dot(hid.astype(BF16), wd_bf[...], preferred_element_type=F32)
        _for_rows(nval, lambda r: pltpu.make_async_copy(
            obuf.at[slot, pl.ds(r, 1)], o_hbm.at[pl.ds(dst_ref[0, 0, r], 1)],
            ssem.at[slot]).start())

        @pl.when(nval_next == 0)
        def _():
            @pl.when(i >= 1)
            def _():
                wait_scatter(other, nval_ref[jnp.maximum(i - 1, 0)])

            wait_scatter(slot, nval)


def _moe_ffn(x, g, plan, wg, wu, wd, layer, tm, n_tiles):
    T, D = x.shape
    tile_expert, tile_nvalid, row_src, row_dst = plan
    f = wg.shape[3]
    row_src = row_src.reshape(n_tiles, 1, tm)

    def idx_spec(step):
        return pl.BlockSpec((1, 1, tm),
                            lambda i, te, nv: (jnp.minimum(i + step, n_tiles - 1), 0, 0),
                            memory_space=pltpu.SMEM)

    grid_spec = pltpu.PrefetchScalarGridSpec(
        num_scalar_prefetch=2,
        grid=(n_tiles,),
        in_specs=[
            idx_spec(0), idx_spec(1), idx_spec(0),
            pl.BlockSpec((1, D), lambda i, te, nv: (0, 0)),
            pl.BlockSpec((None, None, D, f), lambda i, te, nv: (layer, te[i], 0, 0)),
            pl.BlockSpec((None, None, D, f), lambda i, te, nv: (layer, te[i], 0, 0)),
            pl.BlockSpec((None, None, f, D), lambda i, te, nv: (layer, te[i], 0, 0)),
            pl.BlockSpec(memory_space=pl.ANY),
        ],
        out_specs=pl.BlockSpec(memory_space=pl.ANY),
        scratch_shapes=[pltpu.VMEM((2, tm, D), F32), pltpu.VMEM((tm, D), BF16),
                        pltpu.VMEM((2, tm, D), F32),
                        pltpu.VMEM((D, 2 * f), BF16), pltpu.VMEM((f, D), BF16),
                        pltpu.SemaphoreType.DMA((2,)), pltpu.SemaphoreType.DMA((2,))],
    )
    return pl.pallas_call(
        _ffn_kernel,
        grid_spec=grid_spec,
        out_shape=jax.ShapeDtypeStruct((2 * T, D), F32),
        compiler_params=_params(("arbitrary",)),
        name="moe_ffn",
    )(tile_expert, tile_nvalid, row_src, row_src, row_dst.reshape(n_tiles, 1, tm),
      g.reshape(1, D).astype(F32), wg, wu, wd, x)


def _combine_kernel(x_ref, o0_ref, o1_ref, wt_ref, g_ref, y_ref, *, final_norm):
    wt = wt_ref[...]
    y = x_ref[...] + (wt[:, 0:1] * o0_ref[...] + wt[:, 1:2] * o1_ref[...])
    if final_norm:
        y = _norm_rows(y, g_ref[...])
    y_ref[...] = y


def _combine(x, o, wts, g, final_norm, tm):
    T, D = x.shape
    nt = T // tm
    return pl.pallas_call(
        functools.partial(_combine_kernel, final_norm=final_norm),
        grid=(nt,),
        in_specs=[pl.BlockSpec((tm, D), lambda i: (i, 0)),
                  pl.BlockSpec((tm, D), lambda i: (i, 0)),
                  pl.BlockSpec((tm, D), lambda i: (i + nt, 0)),
                  pl.BlockSpec((tm, LANES), lambda i: (i, 0)),
                  pl.BlockSpec((1, D), lambda i: (0, 0))],
        out_specs=pl.BlockSpec((tm, D), lambda i: (i, 0)),
        out_shape=jax.ShapeDtypeStruct((T, D), F32),
        compiler_params=_params(("arbitrary",)),
        name="combine",
    )(x, o, o, wts, g.reshape(1, D).astype(F32))


PROJ_TM, PROJ_TN = 512, 1024
OUT_TM, OUT_TN = 512, 1024
ROUTER_TM = 256
FFN_TM = 256
COMBINE_TM = 256


def kernel(x, norm_attn, w_in, rel_bias, out_norm_dil, out_norm_sb, w_out, norm_ffn,
           w_route_group, b_route_group, w_route_expert, b_route_expert,
           w_gate, w_up, w_down, norm_final):
    B, S, D = x.shape
    T = B * S
    depth = w_in.shape[0]
    n_tiles = 2 * T // FFN_TM + N_EXPERTS
    bias_tab = _dilated_bias_table(rel_bias)
    xt = x.reshape(T, D)
    hd, hs = N_HEADS_DIL, N_HEADS_SB
    w_in_bf = w_in.astype(BF16)
    w_out_bf = w_out.astype(BF16)

    for l in range(depth):
        proj = _norm_matmul([xt], [norm_attn[l]], w_in_bf, l, None, BF16,
                            PROJ_TM, PROJ_TN).reshape(B, S, -1)
        y_a = _dilated_attention(proj, bias_tab, 0, hd, 2 * hd, hd)
        y_b = _sb_attention(proj, 3 * hd, 3 * hd + hs, 3 * hd + 2 * hs, hs)
        xt = _norm_matmul([y_a.reshape(T, -1), y_b.reshape(T, -1)],
                          [out_norm_dil[l], out_norm_sb[l]], w_out_bf, l, xt, F32,
                          OUT_TM, OUT_TN)
        eid, wts, counts = _router(xt, norm_ffn[l], w_route_group[l], b_route_group[l],
                                   w_route_expert[l], b_route_expert[l], ROUTER_TM)
        plan = _dispatch_plan(eid, counts, FFN_TM, n_tiles)
        o = _moe_ffn(xt, norm_ffn[l], plan, w_gate, w_up, w_down, l, FFN_TM, n_tiles)
        xt = _combine(xt, o, wts, norm_final, l == depth - 1, COMBINE_TM)
    return xt.reshape(B, S, D)
```

```python
import functools
import math

import jax
import jax.numpy as jnp
from jax import lax
from jax.experimental import pallas as pl
from jax.experimental.pallas import tpu as pltpu

HEAD_DIM = 128
N_HEADS_DIL = 16
N_HEADS_SB = 16
DILATED_CONFIGS = ((128, 1), (512, 4), (2048, 16))
BLOCK = 128
N_BUCKETS = 32
MAX_DISTANCE = 2048
N_GROUPS = 4
EXPERTS_PER_GROUP = 8
N_EXPERTS = N_GROUPS * EXPERTS_PER_GROUP
D_EXPERT = 256
RMS_EPS = 1e-6
NEG_INF = -1e30

LOG2_E = math.log2(math.e)
LANES = 128
VMEM_LIMIT = 56 * 1024 * 1024

F32 = jnp.float32
BF16 = jnp.bfloat16


def _params(sem, vmem=VMEM_LIMIT):
    return pltpu.CompilerParams(dimension_semantics=sem, vmem_limit_bytes=vmem)


NORM_ROWS = 64


def _norm_rows(x, g):
    var = jnp.mean(x * x, axis=-1, keepdims=True)
    return (x * lax.rsqrt(var + RMS_EPS)) * g


def _norm_matmul_kernel(*refs, n_in, has_res):
    xs = refs[:n_in]
    gs = refs[n_in:2 * n_in]
    w_ref = refs[2 * n_in]
    pos = 2 * n_in + 1
    res_ref = refs[pos] if has_res else None
    pos += int(has_res)
    o_ref, h_ref = refs[pos], refs[pos + 1]

    @pl.when(pl.program_id(1) == 0)
    def _():
        tm = h_ref.shape[0]
        off = 0
        for x_ref, g_ref in zip(xs, gs):
            k = x_ref.shape[1]

            def body(r, _, x_ref=x_ref, g_ref=g_ref, off=off, k=k):
                rows = pl.ds(pl.multiple_of(r * NORM_ROWS, NORM_ROWS), NORM_ROWS)
                h = _norm_rows(x_ref[rows, :], g_ref[...])
                h_ref[rows, off:off + k] = h.astype(BF16)
                return 0

            lax.fori_loop(0, tm // NORM_ROWS, body, 0)
            off += k

    acc = jnp.dot(h_ref[...], w_ref[...], preferred_element_type=F32)
    if has_res:
        acc = res_ref[...] + acc
    o_ref[...] = acc.astype(o_ref.dtype)


def _norm_matmul(xs, gs, w, layer, res, out_dtype, tm, tn):
    T = xs[0].shape[0]
    _, K, N = w.shape
    assert sum(x.shape[1] for x in xs) == K and T % tm == 0 and N % tn == 0
    n_in = len(xs)
    in_specs = [pl.BlockSpec((tm, x.shape[1]), lambda i, j: (i, 0)) for x in xs]
    in_specs += [pl.BlockSpec((1, x.shape[1]), lambda i, j: (0, 0)) for x in xs]
    in_specs += [pl.BlockSpec((None, K, tn), lambda i, j: (layer, 0, j))]
    args = list(xs) + [g.reshape(1, -1).astype(F32) for g in gs] + [w]
    if res is not None:
        in_specs.append(pl.BlockSpec((tm, tn), lambda i, j: (i, j)))
        args.append(res)
    return pl.pallas_call(
        functools.partial(_norm_matmul_kernel, n_in=n_in, has_res=res is not None),
        grid=(T // tm, N // tn),
        in_specs=in_specs,
        out_specs=pl.BlockSpec((tm, tn), lambda i, j: (i, j)),
        out_shape=jax.ShapeDtypeStruct((T, N), out_dtype),
        scratch_shapes=[pltpu.VMEM((tm, K), BF16)],
        compiler_params=_params(("arbitrary", "arbitrary")),
        name="norm_matmul",
    )(*args)


def _t5_bucket(dist):
    max_exact = N_BUCKETS // 2
    d_f = jnp.maximum(dist, 1).astype(F32)
    large = max_exact + (jnp.log(d_f / max_exact) / math.log(MAX_DISTANCE / max_exact)
                         * (N_BUCKETS - max_exact)).astype(jnp.int32)
    large = jnp.minimum(large, N_BUCKETS - 1)
    return jnp.where(dist < max_exact, dist, large)


def _dilated_bias_table(rel_bias):
    qi = jnp.arange(BLOCK)[:, None]
    kj = jnp.arange(2 * BLOCK)[None, :]
    dist = BLOCK + qi - kj
    tabs = []
    for window, dilation in DILATED_CONFIGS:
        band = (dist >= 0) & (dist <= window // dilation)
        bucket = _t5_bucket(jnp.maximum(dist, 0) * dilation)
        onehot = (bucket[:, :, None] == jnp.arange(N_BUCKETS)[None, None, :]).astype(F32)
        bias = jnp.einsum('qkb,bh->hqk', onehot, rel_bias.astype(F32),
                          precision=lax.Precision.HIGHEST)
        tabs.append(jnp.where(band[None], bias, NEG_INF))
    return jnp.stack(tabs, axis=0)


DIL_UNROLL = 8
DIL_PRE_STRIDE = 4


def _strided_rows(start, dilation):
    if dilation == 1:
        return pl.ds(start, BLOCK)
    return pl.ds(start, BLOCK, stride=dilation)


def _dilated_kernel(q_ref, k_ref, v_ref, bias_ref, o_ref, q32, k32, v32, oc, lc,
                    qd, kd, vd, od, ld):
    S = q_ref.shape[0]
    scale = HEAD_DIM ** -0.5
    n_blocks = S // BLOCK
    q32[...] = q_ref[...].astype(F32)
    k32[...] = k_ref[...].astype(F32)
    v32[...] = v_ref[...].astype(F32)

    bpc = n_blocks // DIL_PRE_STRIDE

    def pre_rows(b):
        c = b // bpc
        return pl.ds(c + DIL_PRE_STRIDE * BLOCK * (b - c * bpc), BLOCK, stride=DIL_PRE_STRIDE)

    def to_pre(b, _):
        dst = pl.ds(pl.multiple_of(b * BLOCK, BLOCK), BLOCK)
        qd[dst, :] = q32[pre_rows(b), :]
        kd[dst, :] = k32[pre_rows(b), :]
        vd[dst, :] = v32[pre_rows(b), :]
        return 0

    lax.fori_loop(0, n_blocks, to_pre, 0)

    for ci, (_, dil) in enumerate(DILATED_CONFIGS):
        nb = S // dil // BLOCK
        two_level = dil > DIL_PRE_STRIDE
        stride = dil // DIL_PRE_STRIDE if two_level else dil
        qa, ka, va = (qd, kd, vd) if two_level else (q32, k32, v32)
        out_c, lse_c = (od, ld) if two_level else (oc.at[ci], lc.at[ci])

        def block(idx, ci=ci, dil=dil, nb=nb, two_level=two_level, stride=stride,
                  qa=qa, ka=ka, va=va, out_c=out_c, lse_c=lse_c):
            g = idx // nb
            n = idx - g * nb
            if two_level:
                c = g % DIL_PRE_STRIDE
                first = c * (S // DIL_PRE_STRIDE) + g // DIL_PRE_STRIDE
            else:
                first = g
            start = first + stride * BLOCK * n
            pstart = first + stride * BLOCK * jnp.maximum(n - 1, 0)
            cur = _strided_rows(start, stride)
            prev = _strided_rows(pstart, stride)
            qs = qa[cur, :].astype(BF16)
            kc = jnp.concatenate([ka[prev, :], ka[cur, :]], axis=0).astype(BF16)
            vc = jnp.concatenate([va[prev, :], va[cur, :]], axis=0).astype(BF16)
            z = lax.dot_general(qs, kc, (((1,), (1,)), ((), ())),
                                preferred_element_type=F32) * scale
            bias = bias_ref[ci]
            col = lax.broadcasted_iota(jnp.int32, bias.shape, 1)
            valid = (bias > 0.5 * NEG_INF) & ((col >= BLOCK) | (n > 0))
            logits = jnp.where(valid, z + bias, NEG_INF)
            m = jnp.max(logits, axis=-1, keepdims=True)
            p = jnp.exp(logits - m)
            l = jnp.sum(p, axis=-1, keepdims=True)
            out = jnp.dot(p.astype(BF16), vc, preferred_element_type=F32) / l
            lse = m + jnp.log(l)
            out_c[cur, :] = out
            lse_c[cur, :] = jnp.broadcast_to(lse, (BLOCK, HEAD_DIM))

        def body(t, _, block=block):
            for u in range(DIL_UNROLL):
                block(t * DIL_UNROLL + u)
            return 0

        lax.fori_loop(0, n_blocks // DIL_UNROLL, body, 0)

        if two_level:
            def from_pre(b, _, ci=ci):
                src = pl.ds(pl.multiple_of(b * BLOCK, BLOCK), BLOCK)
                oc[ci, pre_rows(b), :] = od[src, :]
                lc[ci, pre_rows(b), :] = ld[src, :]
                return 0

            lax.fori_loop(0, n_blocks, from_pre, 0)

    def combine(r, _):
        rows = pl.ds(pl.multiple_of(r * BLOCK, BLOCK), BLOCK)
        l0, l1, l2 = lc[0, rows, :], lc[1, rows, :], lc[2, rows, :]
        m = jnp.maximum(jnp.maximum(l0, l1), l2)
        e0, e1, e2 = jnp.exp(l0 - m), jnp.exp(l1 - m), jnp.exp(l2 - m)
        tot = e0 + e1 + e2
        out = (e0 * oc[0, rows, :] + e1 * oc[1, rows, :] + e2 * oc[2, rows, :]) / tot
        o_ref[rows, :] = out.astype(o_ref.dtype)
        return 0

    lax.fori_loop(0, S // BLOCK, combine, 0)


def _dilated_attention(proj, bias_tab, col_q, col_k, col_v, n_heads):
    B, S, _ = proj.shape
    assert len(DILATED_CONFIGS) == 3
    assert all(S % (d * BLOCK) == 0 for _, d in DILATED_CONFIGS)
    assert all(d <= DIL_PRE_STRIDE or d % DIL_PRE_STRIDE == 0 for _, d in DILATED_CONFIGS)
    assert sum(d > DIL_PRE_STRIDE for _, d in DILATED_CONFIGS) <= 1
    assert (S // BLOCK) % DIL_UNROLL == 0 and (S // BLOCK) % DIL_PRE_STRIDE == 0

    def head_spec(col0):
        return pl.BlockSpec((None, S, HEAD_DIM), lambda b, h: (b, 0, col0 + h))

    n_cfg = len(DILATED_CONFIGS)
    return pl.pallas_call(
        _dilated_kernel,
        grid=(B, n_heads),
        in_specs=[head_spec(col_q), head_spec(col_k), head_spec(col_v),
                  pl.BlockSpec((n_cfg, None, BLOCK, 2 * BLOCK), lambda b, h: (0, h, 0, 0))],
        out_specs=pl.BlockSpec((None, S, HEAD_DIM), lambda b, h: (b, 0, h)),
        out_shape=jax.ShapeDtypeStruct((B, S, n_heads * HEAD_DIM), F32),
        scratch_shapes=[pltpu.VMEM((S, HEAD_DIM), F32)] * 3
        + [pltpu.VMEM((n_cfg, S, HEAD_DIM), F32)] * 2
        + [pltpu.VMEM((S, HEAD_DIM), F32)] * 5,
        compiler_params=_params(("arbitrary", "arbitrary")),
        name="dilated_attention",
    )(proj, proj, proj, bias_tab)


SB_TQ = 512
SB_KC = 256
SB_GROUP = SB_TQ // SB_KC


def _sb_kernel(q_ref, k_ref, v_ref, o_ref, tri_ref, acc_ref, carry_ref,
               l1m_ref, lbeta_ref, lb_ref, tot_ref):
    S = q_ref.shape[0]
    n_tiles = S // SB_TQ
    scale = HEAD_DIM ** -0.5
    tri_ref[...] = jnp.where(lax.broadcasted_iota(jnp.int32, (SB_KC, SB_KC), 0)
                             > lax.broadcasted_iota(jnp.int32, (SB_KC, SB_KC), 1),
                             1.0, 0.0).astype(BF16)

    def chunk_start(kbase, u):
        return pl.multiple_of(kbase + SB_TQ - SB_KC * (u + 1), SB_KC)

    def scores(qbase, kbase, par, diag):
        q = q_ref[pl.ds(qbase, SB_TQ), :]
        for u in range(SB_GROUP):
            z = lax.dot_general(q, k_ref[pl.ds(chunk_start(kbase, u), SB_KC), :],
                                (((1,), (1,)), ((), ())),
                                preferred_element_type=F32) * (scale * LOG2_E)
            neg_abs = lax.bitcast_convert_type(
                lax.bitcast_convert_type(z, jnp.uint32) | jnp.uint32(0x80000000), F32)
            lg = jnp.log(1.0 + jnp.exp2(neg_abs)) * LOG2_E
            log_beta = jnp.minimum(z, 0.0) - lg
            log_1m = log_beta - z
            if diag:
                off = SB_TQ - SB_KC * (u + 1)
                causal = (lax.broadcasted_iota(jnp.int32, (SB_TQ, SB_KC), 1) + off
                          < lax.broadcasted_iota(jnp.int32, (SB_TQ, SB_KC), 0))
                log_1m = jnp.where(causal, log_1m, 0.0)
                log_beta = jnp.where(causal, log_beta, -jnp.inf)
            l1m_ref[par, u] = log_1m.astype(BF16)
            lbeta_ref[par, u] = log_beta

    def sums(par):
        for u in range(SB_GROUP):
            l1m = l1m_ref[par, u]
            rem = jnp.dot(l1m, tri_ref[...], preferred_element_type=F32)
            lb_ref[par, u] = lbeta_ref[par, u] + rem
            tot = rem[:, 0:1] + l1m[:, 0:1].astype(F32)
            tot_ref[par, u] = jnp.broadcast_to(tot, (SB_TQ, HEAD_DIM))

    def values(kbase, par):
        carry = carry_ref[...]
        acc = acc_ref[...]
        for u in range(SB_GROUP):
            lb = lb_ref[par, u]
            a = jnp.concatenate([jnp.exp2(lb[:, c * HEAD_DIM:(c + 1) * HEAD_DIM] + carry)
                                 for c in range(SB_KC // HEAD_DIM)], axis=1)
            acc = acc + jnp.dot(a.astype(BF16), v_ref[pl.ds(chunk_start(kbase, u), SB_KC), :],
                                preferred_element_type=F32)
            carry = carry + tot_ref[par, u]
        carry_ref[...] = carry
        acc_ref[...] = acc

    def start_tile(qbase):
        acc_ref[...] = jnp.zeros_like(acc_ref)
        carry_ref[...] = jnp.zeros_like(carry_ref)
        scores(qbase, qbase, 0, True)
        sums(0)

    start_tile(0)
    values(0, 0)
    o_ref[pl.ds(0, SB_TQ), :] = acc_ref[...].astype(o_ref.dtype)

    def q_tile(i, _):
        qbase = pl.multiple_of(i * SB_TQ, SB_TQ)

        def kbase_of(g):
            return pl.multiple_of((i - g) * SB_TQ, SB_TQ)

        start_tile(qbase)
        scores(qbase, kbase_of(1), 1, False)

        def step(s, par):
            values(kbase_of(s - 2), par)
            sums(1 - par)
            scores(qbase, kbase_of(s), par, False)

        def two_steps(t, _):
            step(2 + 2 * t, 0)
            step(3 + 2 * t, 1)
            return 0

        n_steps = i - 1
        lax.fori_loop(0, n_steps // 2, two_steps, 0)

        @pl.when(n_steps % 2 == 1)
        def _():
            step(i, 0)
        last = i % 2
        values(kbase_of(i - 1), 1 - last)
        sums(last)
        values(kbase_of(i), last)
        o_ref[pl.ds(qbase, SB_TQ), :] = acc_ref[...].astype(o_ref.dtype)
        return 0

    lax.fori_loop(1, n_tiles, q_tile, 0)


def _sb_attention(proj, col_q, col_k, col_v, n_heads):
    B, S, _ = proj.shape
    assert S % SB_TQ == 0 and SB_TQ % SB_KC == 0 and SB_KC % HEAD_DIM == 0

    def head_spec(col0):
        return pl.BlockSpec((None, S, HEAD_DIM), lambda b, h: (b, 0, col0 + h))

    stage = (2, SB_GROUP, SB_TQ, SB_KC)
    return pl.pallas_call(
        _sb_kernel,
        grid=(B, n_heads),
        in_specs=[head_spec(col_q), head_spec(col_k), head_spec(col_v)],
        out_specs=pl.BlockSpec((None, S, HEAD_DIM), lambda b, h: (b, 0, h)),
        out_shape=jax.ShapeDtypeStruct((B, S, n_heads * HEAD_DIM), F32),
        scratch_shapes=[pltpu.VMEM((SB_KC, SB_KC), BF16),
                        pltpu.VMEM((SB_TQ, HEAD_DIM), F32), pltpu.VMEM((SB_TQ, HEAD_DIM), F32),
                        pltpu.VMEM(stage, BF16), pltpu.VMEM(stage, F32), pltpu.VMEM(stage, F32),
                        pltpu.VMEM((2, SB_GROUP, SB_TQ, HEAD_DIM), F32)],
        compiler_params=_params(("arbitrary", "arbitrary")),
        name="sb_attention",
    )(proj, proj, proj)


def _router_kernel(x_ref, g_ref, wr_ref, br_ref, id_ref, wt_ref, cnt_ref):
    h = _norm_rows(x_ref[...], g_ref[...])
    logits = jnp.dot(h, wr_ref[...], precision=lax.Precision.HIGHEST,
                     preferred_element_type=F32) + br_ref[...]
    lane = lax.broadcasted_iota(jnp.int32, logits.shape, 1).astype(F32)
    big = float(LANES)
    ninf = -jnp.inf

    def first_argmax(vals):
        top = jnp.max(vals, axis=-1, keepdims=True)
        idx = jnp.min(jnp.where(vals == top, lane, big), axis=-1, keepdims=True)
        return top, idx

    gl = jnp.where(lane < N_GROUPS, logits, ninf)
    gmax, g_top = first_argmax(gl)
    p_g = 1.0 / jnp.sum(jnp.exp(gl - gmax), axis=-1, keepdims=True)
    first = N_GROUPS + g_top * EXPERTS_PER_GROUP
    el = jnp.where((lane >= first) & (lane < first + EXPERTS_PER_GROUP), logits, ninf)
    v1, i1 = first_argmax(el)
    v2, i2 = first_argmax(jnp.where(lane == i1, ninf, el))
    t = jnp.exp(v2 - v1)
    w1 = p_g / (1.0 + t)
    w2 = p_g * t / (1.0 + t)
    ids = jnp.where(lane == 0, i1 - N_GROUPS, jnp.where(lane == 1, i2 - N_GROUPS, 0.0))
    id_ref[...] = ids.astype(jnp.int32)
    wt_ref[...] = jnp.where(lane == 0, w1, jnp.where(lane == 1, w2, 0.0))

    @pl.when(pl.program_id(0) == 0)
    def _():
        cnt_ref[...] = jnp.zeros_like(cnt_ref)

    chosen = jnp.where((lane == i1) | (lane == i2), 1.0, 0.0)
    cnt_ref[...] += jnp.sum(chosen, axis=0, keepdims=True)


def _router(x, g, w_rg, b_rg, w_re, b_re, tm):
    T, D = x.shape
    n_logits = N_GROUPS + N_EXPERTS
    wr = jnp.concatenate([w_rg, jnp.transpose(w_re, (1, 0, 2)).reshape(D, N_EXPERTS)], axis=1)
    wr = jnp.pad(wr.astype(F32), ((0, 0), (0, LANES - n_logits)))
    br = jnp.pad(jnp.concatenate([b_rg, b_re.reshape(-1)]).astype(F32), (0, LANES - n_logits))
    eid, wts, cnt = pl.pallas_call(
        _router_kernel,
        grid=(T // tm,),
        in_specs=[pl.BlockSpec((tm, D), lambda i: (i, 0)),
                  pl.BlockSpec((1, D), lambda i: (0, 0)),
                  pl.BlockSpec((D, LANES), lambda i: (0, 0)),
                  pl.BlockSpec((1, LANES), lambda i: (0, 0))],
        out_specs=[pl.BlockSpec((tm, LANES), lambda i: (i, 0))] * 2
        + [pl.BlockSpec((1, LANES), lambda i: (0, 0))],
        out_shape=[jax.ShapeDtypeStruct((T, LANES), jnp.int32),
                   jax.ShapeDtypeStruct((T, LANES), F32),
                   jax.ShapeDtypeStruct((1, LANES), F32)],
        compiler_params=_params(("arbitrary",)),
        name="router",
    )(x, g.reshape(1, D).astype(F32), wr, br.reshape(1, LANES))
    counts = cnt[0, N_GROUPS:N_GROUPS + N_EXPERTS].astype(jnp.int32)
    return eid[:, :2], wts, counts


def _dispatch_plan(eid, counts, tm, n_tiles):
    T = eid.shape[0]
    n_assign = 2 * T
    order = jnp.argsort(eid.T.reshape(-1), stable=True).astype(jnp.int32)
    tiles_per = (counts + tm - 1) // tm
    tile_end = jnp.cumsum(tiles_per)
    tile_idx = jnp.arange(n_tiles, dtype=jnp.int32)
    tile_expert = jnp.minimum(
        jnp.sum((tile_end[None, :] <= tile_idx[:, None]).astype(jnp.int32), axis=1), N_EXPERTS - 1)
    onehot = (tile_expert[:, None] == jnp.arange(N_EXPERTS, dtype=jnp.int32)[None, :])

    def of_tile(per_expert):
        return jnp.sum(jnp.where(onehot, per_expert[None, :], 0), axis=1)

    local = (tile_idx - of_tile(tile_end - tiles_per)) * tm
    tile_nvalid = jnp.where(tile_idx < tile_end[-1], jnp.clip(of_tile(counts) - local, 0, tm), 0)

    seg_start = jnp.cumsum(counts) - counts
    row_start = (tile_end - tiles_per) * tm
    order_ext = jnp.concatenate([order, jnp.zeros((n_assign,), jnp.int32)])

    def place(e, rows):
        seg = lax.dynamic_slice(order_ext, (seg_start[e],), (n_assign,))
        return lax.dynamic_update_slice(rows, seg, (row_start[e],))

    rows = lax.fori_loop(0, N_EXPERTS, place, jnp.zeros((n_tiles * tm + n_assign,), jnp.int32))
    ids = rows[:n_tiles * tm].reshape(n_tiles, tm)
    row_src = jnp.where(ids >= T, ids - T, ids)
    return tile_expert, tile_nvalid.astype(jnp.int32), row_src, ids


DMA_ISSUE_UNROLL = 8


def _for_rows(n, fn):
    n_groups = n // DMA_ISSUE_UNROLL

    def group(gi, _):
        for u in range(DMA_ISSUE_UNROLL):
            fn(gi * DMA_ISSUE_UNROLL + u)
        return 0

    def single(r, _):
        fn(r)
        return 0

    lax.fori_loop(0, n_groups, group, 0)
    lax.fori_loop(n_groups * DMA_ISSUE_UNROLL, n, single, 0)


def _wait_rows(n, max_rows, copy_of_rows):
    k = max_rows
    while k >= 1:
        @pl.when((n & k) != 0)
        def _(k=k):
            copy_of_rows(k).wait()
        k //= 2


def _ffn_kernel(texp_ref, nval_ref, src_ref, src_next_ref, dst_ref, g_ref, wg_ref, wu_ref, wd_ref,
                x_hbm, o_hbm, xbuf, hbuf, obuf, wgu_bf, wd_bf, gsem, ssem):
    i = pl.program_id(0)
    n_tiles = pl.num_programs(0)
    tm = xbuf.shape[1]
    f = wd_ref.shape[0]
    slot = i % 2
    other = 1 - slot
    nval = nval_ref[i]
    nval_next = jnp.where(i + 1 < n_tiles, nval_ref[jnp.minimum(i + 1, n_tiles - 1)], 0)

    def start_gather(idx_ref, s, n):
        _for_rows(n, lambda r: pltpu.make_async_copy(
            x_hbm.at[pl.ds(idx_ref[0, 0, r], 1)], xbuf.at[s, pl.ds(r, 1)], gsem.at[s]).start())

    def wait_gather(s, n):
        _wait_rows(n, tm, lambda k: pltpu.make_async_copy(
            x_hbm.at[pl.ds(0, k)], xbuf.at[s, pl.ds(0, k)], gsem.at[s]))

    def wait_scatter(s, n):
        _wait_rows(n, tm, lambda k: pltpu.make_async_copy(
            obuf.at[s, pl.ds(0, k)], o_hbm.at[pl.ds(0, k)], ssem.at[s]))

    @pl.when(i == 0)
    def _():
        xbuf[...] = jnp.zeros_like(xbuf)
        start_gather(src_ref, 0, nval)

    @pl.when(nval > 0)
    def _():
        start_gather(src_next_ref, other, nval_next)
        wait_gather(slot, nval)

        @pl.when(i >= 2)
        def _():
            wait_scatter(slot, nval_ref[jnp.maximum(i - 2, 0)])

        def norm(r, _):
            rows = pl.ds(pl.multiple_of(r * NORM_ROWS, NORM_ROWS), NORM_ROWS)
            hbuf[rows, :] = _norm_rows(xbuf[slot, rows, :], g_ref[...]).astype(BF16)
            return 0

        lax.fori_loop(0, tm // NORM_ROWS, norm, 0)

        @pl.when((i == 0) | (texp_ref[i] != texp_ref[jnp.maximum(i - 1, 0)]))
        def _():
            wgu_bf[:, :f] = wg_ref[...].astype(BF16)
            wgu_bf[:, f:] = wu_ref[...].astype(BF16)
            wd_bf[...] = wd_ref[...].astype(BF16)

        h = jnp.dot(hbuf[...], wgu_bf[...], preferred_element_type=F32)
        hg, hu = h[:, :f], h[:, f:]
        hid = (hg * (1.0 / (1.0 + jnp.exp(-hg)))) * hu
        obuf[slot] = jnp.dot(hid.astype(BF16), wd_bf[...], preferred_element_type=F32)
        _for_rows(nval, lambda r: pltpu.make_async_copy(
            obuf.at[slot, pl.ds(r, 1)], o_hbm.at[pl.ds(dst_ref[0, 0, r], 1)],
            ssem.at[slot]).start())

        @pl.when(nval_next == 0)
        def _():
            @pl.when(i >= 1)
            def _():
                wait_scatter(other, nval_ref[jnp.maximum(i - 1, 0)])

            wait_scatter(slot, nval)


def _moe_ffn(x, g, plan, wg, wu, wd, layer, tm, n_tiles):
    T, D = x.shape
    tile_expert, tile_nvalid, row_src, row_dst = plan
    f = wg.shape[3]
    row_src = row_src.reshape(n_tiles, 1, tm)

    def idx_spec(step):
        return pl.BlockSpec((1, 1, tm),
                            lambda i, te, nv: (jnp.minimum(i + step, n_tiles - 1), 0, 0),
                            memory_space=pltpu.SMEM)

    grid_spec = pltpu.PrefetchScalarGridSpec(
        num_scalar_prefetch=2,
        grid=(n_tiles,),
        in_specs=[
            idx_spec(0), idx_spec(1), idx_spec(0),
            pl.BlockSpec((1, D), lambda i, te, nv: (0, 0)),
            pl.BlockSpec((None, None, D, f), lambda i, te, nv: (layer, te[i], 0, 0)),
            pl.BlockSpec((None, None, D, f), lambda i, te, nv: (layer, te[i], 0, 0)),
            pl.BlockSpec((None, None, f, D), lambda i, te, nv: (layer, te[i], 0, 0)),
            pl.BlockSpec(memory_space=pl.ANY),
        ],
        out_specs=pl.BlockSpec(memory_space=pl.ANY),
        scratch_shapes=[pltpu.VMEM((2, tm, D), F32), pltpu.VMEM((tm, D), BF16),
                        pltpu.VMEM((2, tm, D), F32),
                        pltpu.VMEM((D, 2 * f), BF16), pltpu.VMEM((f, D), BF16),
                        pltpu.SemaphoreType.DMA((2,)), pltpu.SemaphoreType.DMA((2,))],
    )
    return pl.pallas_call(
        _ffn_kernel,
        grid_spec=grid_spec,
        out_shape=jax.ShapeDtypeStruct((2 * T, D), F32),
        compiler_params=_params(("arbitrary",)),
        name="moe_ffn",
    )(tile_expert, tile_nvalid, row_src, row_src, row_dst.reshape(n_tiles, 1, tm),
      g.reshape(1, D).astype(F32), wg, wu, wd, x)


def _combine_kernel(x_ref, o0_ref, o1_ref, wt_ref, g_ref, y_ref, *, final_norm):
    wt = wt_ref[...]
    y = x_ref[...] + (wt[:, 0:1] * o0_ref[...] + wt[:, 1:2] * o1_ref[...])
    if final_norm:
        y = _norm_rows(y, g_ref[...])
    y_ref[...] = y


def _combine(x, o, wts, g, final_norm, tm):
    T, D = x.shape
    nt = T // tm
    return pl.pallas_call(
        functools.partial(_combine_kernel, final_norm=final_norm),
        grid=(nt,),
        in_specs=[pl.BlockSpec((tm, D), lambda i: (i, 0)),
                  pl.BlockSpec((tm, D), lambda i: (i, 0)),
                  pl.BlockSpec((tm, D), lambda i: (i + nt, 0)),
                  pl.BlockSpec((tm, LANES), lambda i: (i, 0)),
                  pl.BlockSpec((1, D), lambda i: (0, 0))],
        out_specs=pl.BlockSpec((tm, D), lambda i: (i, 0)),
        out_shape=jax.ShapeDtypeStruct((T, D), F32),
        compiler_params=_params(("arbitrary",)),
        name="combine",
    )(x, o, o, wts, g.reshape(1, D).astype(F32))


PROJ_TM, PROJ_TN = 512, 1024
OUT_TM, OUT_TN = 512, 1024
ROUTER_TM = 256
FFN_TM = 256
COMBINE_TM = 256


def kernel(x, norm_attn, w_in, rel_bias, out_norm_dil, out_norm_sb, w_out, norm_ffn,
           w_route_group, b_route_group, w_route_expert, b_route_expert,
           w_gate, w_up, w_down, norm_final):
    B, S, D = x.shape
    T = B * S
    depth = w_in.shape[0]
    n_tiles = 2 * T // FFN_TM + N_EXPERTS
    bias_tab = _dilated_bias_table(rel_bias)
    xt = x.reshape(T, D)
    hd, hs = N_HEADS_DIL, N_HEADS_SB
    w_in_bf = w_in.astype(BF16)
    w_out_bf = w_out.astype(BF16)

    for l in range(depth):
        proj = _norm_matmul([xt], [norm_attn[l]], w_in_bf, l, None, BF16,
                            PROJ_TM, PROJ_TN).reshape(B, S, -1)
        y_a = _dilated_attention(proj, bias_tab, 0, hd, 2 * hd, hd)
        y_b = _sb_attention(proj, 3 * hd, 3 * hd + hs, 3 * hd + 2 * hs, hs)
        xt = _norm_matmul([y_a.reshape(T, -1), y_b.reshape(T, -1)],
                          [out_norm_dil[l], out_norm_sb[l]], w_out_bf, l, xt, F32,
                          OUT_TM, OUT_TN)
        eid, wts, counts = _router(xt, norm_ffn[l], w_route_group[l], b_route_group[l],
                                   w_route_expert[l], b_route_expert[l], ROUTER_TM)
        plan = _dispatch_plan(eid, counts, FFN_TM, n_tiles)
        o = _moe_ffn(xt, norm_ffn[l], plan, w_gate, w_up, w_down, l, FFN_TM, n_tiles)
        xt = _combine(xt, o, wts, norm_final, l == depth - 1, COMBINE_TM)
    return xt.reshape(B, S, D)
```

```python
import functools
import math

import jax
import jax.numpy as jnp
from jax import lax
from jax.experimental import pallas as pl
from jax.experimental.pallas import tpu as pltpu

HEAD_DIM = 128
N_HEADS_DIL = 16
N_HEADS_SB = 16
DILATED_CONFIGS = ((128, 1), (512, 4), (2048, 16))
BLOCK = 128
N_BUCKETS = 32
MAX_DISTANCE = 2048
N_GROUPS = 4
EXPERTS_PER_GROUP = 8
N_EXPERTS = N_GROUPS * EXPERTS_PER_GROUP
D_EXPERT = 256
RMS_EPS = 1e-6
NEG_INF = -1e30

LOG2_E = math.log2(math.e)
LANES = 128
VMEM_LIMIT = 56 * 1024 * 1024

F32 = jnp.float32
BF16 = jnp.bfloat16


def _params(sem, vmem=VMEM_LIMIT):
    return pltpu.CompilerParams(dimension_semantics=sem, vmem_limit_bytes=vmem)


NORM_ROWS = 64


def _norm_rows(x, g):
    var = jnp.mean(x * x, axis=-1, keepdims=True)
    return (x * lax.rsqrt(var + RMS_EPS)) * g


def _norm_matmul_kernel(*refs, n_in, has_res):
    xs = refs[:n_in]
    gs = refs[n_in:2 * n_in]
    w_ref = refs[2 * n_in]
    pos = 2 * n_in + 1
    res_ref = refs[pos] if has_res else None
    pos += int(has_res)
    o_ref, h_ref = refs[pos], refs[pos + 1]

    @pl.when(pl.program_id(1) == 0)
    def _():
        tm = h_ref.shape[0]
        off = 0
        for x_ref, g_ref in zip(xs, gs):
            k = x_ref.shape[1]

            def body(r, _, x_ref=x_ref, g_ref=g_ref, off=off, k=k):
                rows = pl.ds(pl.multiple_of(r * NORM_ROWS, NORM_ROWS), NORM_ROWS)
                h = _norm_rows(x_ref[rows, :], g_ref[...])
                h_ref[rows, off:off + k] = h.astype(BF16)
                return 0

            lax.fori_loop(0, tm // NORM_ROWS, body, 0)
            off += k

    acc = jnp.dot(h_ref[...], w_ref[...], preferred_element_type=F32)
    if has_res:
        acc = res_ref[...] + acc
    o_ref[...] = acc.astype(o_ref.dtype)


def _norm_matmul(xs, gs, w, layer, res, out_dtype, tm, tn):
    T = xs[0].shape[0]
    _, K, N = w.shape
    assert sum(x.shape[1] for x in xs) == K and T % tm == 0 and N % tn == 0
    n_in = len(xs)
    in_specs = [pl.BlockSpec((tm, x.shape[1]), lambda i, j: (i, 0)) for x in xs]
    in_specs += [pl.BlockSpec((1, x.shape[1]), lambda i, j: (0, 0)) for x in xs]
    in_specs += [pl.BlockSpec((None, K, tn), lambda i, j: (layer, 0, j))]
    args = list(xs) + [g.reshape(1, -1).astype(F32) for g in gs] + [w]
    if res is not None:
        in_specs.append(pl.BlockSpec((tm, tn), lambda i, j: (i, j)))
        args.append(res)
    return pl.pallas_call(
        functools.partial(_norm_matmul_kernel, n_in=n_in, has_res=res is not None),
        grid=(T // tm, N // tn),
        in_specs=in_specs,
        out_specs=pl.BlockSpec((tm, tn), lambda i, j: (i, j)),
        out_shape=jax.ShapeDtypeStruct((T, N), out_dtype),
        scratch_shapes=[pltpu.VMEM((tm, K), BF16)],
        compiler_params=_params(("arbitrary", "arbitrary")),
        name="norm_matmul",
    )(*args)


def _t5_bucket(dist):
    max_exact = N_BUCKETS // 2
    d_f = jnp.maximum(dist, 1).astype(F32)
    large = max_exact + (jnp.log(d_f / max_exact) / math.log(MAX_DISTANCE / max_exact)
                         * (N_BUCKETS - max_exact)).astype(jnp.int32)
    large = jnp.minimum(large, N_BUCKETS - 1)
    return jnp.where(dist < max_exact, dist, large)


def _dilated_bias_table(rel_bias):
    qi = jnp.arange(BLOCK)[:, None]
    kj = jnp.arange(2 * BLOCK)[None, :]
    dist = BLOCK + qi - kj
    tabs = []
    for window, dilation in DILATED_CONFIGS:
        band = (dist >= 0) & (dist <= window // dilation)
        bucket = _t5_bucket(jnp.maximum(dist, 0) * dilation)
        onehot = (bucket[:, :, None] == jnp.arange(N_BUCKETS)[None, None, :]).astype(F32)
        bias = jnp.einsum('qkb,bh->hqk', onehot, rel_bias.astype(F32),
                          precision=lax.Precision.HIGHEST)
        tabs.append(jnp.where(band[None], bias, NEG_INF))
    return jnp.stack(tabs, axis=0)


DIL_UNROLL = 8
DIL_PRE_STRIDE = 4


def _strided_rows(start, dilation):
    if dilation == 1:
        return pl.ds(start, BLOCK)
    return pl.ds(start, BLOCK, stride=dilation)


def _dilated_kernel(q_ref, k_ref, v_ref, bias_ref, o_ref, q32, k32, v32, oc, lc,
                    qd, kd, vd, od, ld):
    S = q_ref.shape[0]
    scale = HEAD_DIM ** -0.5
    n_blocks = S // BLOCK
    q32[...] = q_ref[...].astype(F32)
    k32[...] = k_ref[...].astype(F32)
    v32[...] = v_ref[...].astype(F32)

    bpc = n_blocks // DIL_PRE_STRIDE

    def pre_rows(b):
        c = b // bpc
        return pl.ds(c + DIL_PRE_STRIDE * BLOCK * (b - c * bpc), BLOCK, stride=DIL_PRE_STRIDE)

    def to_pre(b, _):
        dst = pl.ds(pl.multiple_of(b * BLOCK, BLOCK), BLOCK)
        qd[dst, :] = q32[pre_rows(b), :]
        kd[dst, :] = k32[pre_rows(b), :]
        vd[dst, :] = v32[pre_rows(b), :]
        return 0

    lax.fori_loop(0, n_blocks, to_pre, 0)

    for ci, (_, dil) in enumerate(DILATED_CONFIGS):
        nb = S // dil // BLOCK
        two_level = dil > DIL_PRE_STRIDE
        stride = dil // DIL_PRE_STRIDE if two_level else dil
        qa, ka, va = (qd, kd, vd) if two_level else (q32, k32, v32)
        out_c, lse_c = (od, ld) if two_level else (oc.at[ci], lc.at[ci])

        def block(idx, ci=ci, dil=dil, nb=nb, two_level=two_level, stride=stride,
                  qa=qa, ka=ka, va=va, out_c=out_c, lse_c=lse_c):
            g = idx // nb
            n = idx - g * nb
            if two_level:
                c = g % DIL_PRE_STRIDE
                first = c * (S // DIL_PRE_STRIDE) + g // DIL_PRE_STRIDE
            else:
                first = g
            start = first + stride * BLOCK * n
            pstart = first + stride * BLOCK * jnp.maximum(n - 1, 0)
            cur = _strided_rows(start, stride)
            prev = _strided_rows(pstart, stride)
            qs = qa[cur, :].astype(BF16)
            kc = jnp.concatenate([ka[prev, :], ka[cur, :]], axis=0).astype(BF16)
            vc = jnp.concatenate([va[prev, :], va[cur, :]], axis=0).astype(BF16)
            z = lax.dot_general(qs, kc, (((1,), (1,)), ((), ())),
                                preferred_element_type=F32) * scale
            bias = bias_ref[ci]
            col = lax.broadcasted_iota(jnp.int32, bias.shape, 1)
            valid = (bias > 0.5 * NEG_INF) & ((col >= BLOCK) | (n > 0))
            logits = jnp.where(valid, z + bias, NEG_INF)
            m = jnp.max(logits, axis=-1, keepdims=True)
            p = jnp.exp(logits - m)
            l = jnp.sum(p, axis=-1, keepdims=True)
            out = jnp.dot(p.astype(BF16), vc, preferred_element_type=F32) / l
            lse = m + jnp.log(l)
            out_c[cur, :] = out
            lse_c[cur, :] = jnp.broadcast_to(lse, (BLOCK, HEAD_DIM))

        def body(t, _, block=block):
            for u in range(DIL_UNROLL):
                block(t * DIL_UNROLL + u)
            return 0

        lax.fori_loop(0, n_blocks // DIL_UNROLL, body, 0)

        if two_level:
            def from_pre(b, _, ci=ci):
                src = pl.ds(pl.multiple_of(b * BLOCK, BLOCK), BLOCK)
                oc[ci, pre_rows(b), :] = od[src, :]
                lc[ci, pre_rows(b), :] = ld[src, :]
                return 0

            lax.fori_loop(0, n_blocks, from_pre, 0)

    def combine(r, _):
        rows = pl.ds(pl.multiple_of(r * BLOCK, BLOCK), BLOCK)
        l0, l1, l2 = lc[0, rows, :], lc[1, rows, :], lc[2, rows, :]
        m = jnp.maximum(jnp.maximum(l0, l1), l2)
        e0, e1, e2 = jnp.exp(l0 - m), jnp.exp(l1 - m), jnp.exp(l2 - m)
        tot = e0 + e1 + e2
        out = (e0 * oc[0, rows, :] + e1 * oc[1, rows, :] + e2 * oc[2, rows, :]) / tot
        o_ref[rows, :] = out.astype(o_ref.dtype)
        return 0

    lax.fori_loop(0, S // BLOCK, combine, 0)


def _dilated_attention(proj, bias_tab, col_q, col_k, col_v, n_heads):
    B, S, _ = proj.shape
    assert len(DILATED_CONFIGS) == 3
    assert all(S % (d * BLOCK) == 0 for _, d in DILATED_CONFIGS)
    assert all(d <= DIL_PRE_STRIDE or d % DIL_PRE_STRIDE == 0 for _, d in DILATED_CONFIGS)
    assert sum(d > DIL_PRE_STRIDE for _, d in DILATED_CONFIGS) <= 1
    assert (S // BLOCK) % DIL_UNROLL == 0 and (S // BLOCK) % DIL_PRE_STRIDE == 0

    def head_spec(col0):
        return pl.BlockSpec((None, S, HEAD_DIM), lambda b, h: (b, 0, col0 + h))

    n_cfg = len(DILATED_CONFIGS)
    return pl.pallas_call(
        _dilated_kernel,
        grid=(B, n_heads),
        in_specs=[head_spec(col_q), head_spec(col_k), head_spec(col_v),
                  pl.BlockSpec((n_cfg, None, BLOCK, 2 * BLOCK), lambda b, h: (0, h, 0, 0))],
        out_specs=pl.BlockSpec((None, S, HEAD_DIM), lambda b, h: (b, 0, h)),
        out_shape=jax.ShapeDtypeStruct((B, S, n_heads * HEAD_DIM), F32),
        scratch_shapes=[pltpu.VMEM((S, HEAD_DIM), F32)] * 3
        + [pltpu.VMEM((n_cfg, S, HEAD_DIM), F32)] * 2
        + [pltpu.VMEM((S, HEAD_DIM), F32)] * 5,
        compiler_params=_params(("arbitrary", "arbitrary")),
        name="dilated_attention",
    )(proj, proj, proj, bias_tab)


SB_TQ = 512
SB_KC = 256
SB_GROUP = SB_TQ // SB_KC


def _sb_kernel(q_ref, k_ref, v_ref, o_ref, tri_ref, acc_ref, carry_ref,
               l1m_ref, lbeta_ref, lb_ref, tot_ref):
    S = q_ref.shape[0]
    n_tiles = S // SB_TQ
    scale = HEAD_DIM ** -0.5
    tri_ref[...] = jnp.where(lax.broadcasted_iota(jnp.int32, (SB_KC, SB_KC), 0)
                             > lax.broadcasted_iota(jnp.int32, (SB_KC, SB_KC), 1),
                             1.0, 0.0).astype(BF16)

    def chunk_start(kbase, u):
        return pl.multiple_of(kbase + SB_TQ - SB_KC * (u + 1), SB_KC)

    def scores(qbase, kbase, par, diag):
        q = q_ref[pl.ds(qbase, SB_TQ), :]
        for u in range(SB_GROUP):
            z = lax.dot_general(q, k_ref[pl.ds(chunk_start(kbase, u), SB_KC), :],
                                (((1,), (1,)), ((), ())),
                                preferred_element_type=F32) * (scale * LOG2_E)
            neg_abs = lax.bitcast_convert_type(
                lax.bitcast_convert_type(z, jnp.uint32) | jnp.uint32(0x80000000), F32)
            lg = jnp.log(1.0 + jnp.exp2(neg_abs)) * LOG2_E
            log_beta = jnp.minimum(z, 0.0) - lg
            log_1m = log_beta - z
            if diag:
                off = SB_TQ - SB_KC * (u + 1)
                causal = (lax.broadcasted_iota(jnp.int32, (SB_TQ, SB_KC), 1) + off
                          < lax.broadcasted_iota(jnp.int32, (SB_TQ, SB_KC), 0))
                log_1m = jnp.where(causal, log_1m, 0.0)
                log_beta = jnp.where(causal, log_beta, -jnp.inf)
            l1m_ref[par, u] = log_1m.astype(BF16)
            lbeta_ref[par, u] = log_beta

    def sums(par):
        for u in range(SB_GROUP):
            l1m = l1m_ref[par, u]
            rem = jnp.dot(l1m, tri_ref[...], preferred_element_type=F32)
            lb_ref[par, u] = lbeta_ref[par, u] + rem
            tot = rem[:, 0:1] + l1m[:, 0:1].astype(F32)
            tot_ref[par, u] = jnp.broadcast_to(tot, (SB_TQ, HEAD_DIM))

    def values(kbase, par):
        carry = carry_ref[...]
        acc = acc_ref[...]
        for u in range(SB_GROUP):
            lb = lb_ref[par, u]
            a = jnp.concatenate([jnp.exp2(lb[:, c * HEAD_DIM:(c + 1) * HEAD_DIM] + carry)
                                 for c in range(SB_KC // HEAD_DIM)], axis=1)
            acc = acc + jnp.dot(a.astype(BF16), v_ref[pl.ds(chunk_start(kbase, u), SB_KC), :],
                                preferred_element_type=F32)
            carry = carry + tot_ref[par, u]
        carry_ref[...] = carry
        acc_ref[...] = acc

    def start_tile(qbase):
        acc_ref[...] = jnp.zeros_like(acc_ref)
        carry_ref[...] = jnp.zeros_like(carry_ref)
        scores(qbase, qbase, 0, True)
        sums(0)

    start_tile(0)
    values(0, 0)
    o_ref[pl.ds(0, SB_TQ), :] = acc_ref[...].astype(o_ref.dtype)

    def q_tile(i, _):
        qbase = pl.multiple_of(i * SB_TQ, SB_TQ)

        def kbase_of(g):
            return pl.multiple_of((i - g) * SB_TQ, SB_TQ)

        start_tile(qbase)
        scores(qbase, kbase_of(1), 1, False)

        def step(s, par):
            values(kbase_of(s - 2), par)
            sums(1 - par)
            scores(qbase, kbase_of(s), par, False)

        def two_steps(t, _):
            step(2 + 2 * t, 0)
            step(3 + 2 * t, 1)
            return 0

        n_steps = i - 1
        lax.fori_loop(0, n_steps // 2, two_steps, 0)

        @pl.when(n_steps % 2 == 1)
        def _():
            step(i, 0)
        last = i % 2
        values(kbase_of(i - 1), 1 - last)
        sums(last)
        values(kbase_of(i), last)
        o_ref[pl.ds(qbase, SB_TQ), :] = acc_ref[...].astype(o_ref.dtype)
        return 0

    lax.fori_loop(1, n_tiles, q_tile, 0)


def _sb_attention(proj, col_q, col_k, col_v, n_heads):
    B, S, _ = proj.shape
    assert S % SB_TQ == 0 and SB_TQ % SB_KC == 0 and SB_KC % HEAD_DIM == 0

    def head_spec(col0):
        return pl.BlockSpec((None, S, HEAD_DIM), lambda b, h: (b, 0, col0 + h))

    stage = (2, SB_GROUP, SB_TQ, SB_KC)
    return pl.pallas_call(
        _sb_kernel,
        grid=(B, n_heads),
        in_specs=[head_spec(col_q), head_spec(col_k), head_spec(col_v)],
        out_specs=pl.BlockSpec((None, S, HEAD_DIM), lambda b, h: (b, 0, h)),
        out_shape=jax.ShapeDtypeStruct((B, S, n_heads * HEAD_DIM), F32),
        scratch_shapes=[pltpu.VMEM((SB_KC, SB_KC), BF16),
                        pltpu.VMEM((SB_TQ, HEAD_DIM), F32), pltpu.VMEM((SB_TQ, HEAD_DIM), F32),
                        pltpu.VMEM(stage, BF16), pltpu.VMEM(stage, F32), pltpu.VMEM(stage, F32),
                        pltpu.VMEM((2, SB_GROUP, SB_TQ, HEAD_DIM), F32)],
        compiler_params=_params(("arbitrary", "arbitrary")),
        name="sb_attention",
    )(proj, proj, proj)


def _router_kernel(x_ref, g_ref, wr_ref, br_ref, id_ref, wt_ref, cnt_ref):
    h = _norm_rows(x_ref[...], g_ref[...])
    logits = jnp.dot(h, wr_ref[...], precision=lax.Precision.HIGHEST,
                     preferred_element_type=F32) + br_ref[...]
    lane = lax.broadcasted_iota(jnp.int32, logits.shape, 1).astype(F32)
    big = float(LANES)
    ninf = -jnp.inf

    def first_argmax(vals):
        top = jnp.max(vals, axis=-1, keepdims=True)
        idx = jnp.min(jnp.where(vals == top, lane, big), axis=-1, keepdims=True)
        return top, idx

    gl = jnp.where(lane < N_GROUPS, logits, ninf)
    gmax, g_top = first_argmax(gl)
    p_g = 1.0 / jnp.sum(jnp.exp(gl - gmax), axis=-1, keepdims=True)
    first = N_GROUPS + g_top * EXPERTS_PER_GROUP
    el = jnp.where((lane >= first) & (lane < first + EXPERTS_PER_GROUP), logits, ninf)
    v1, i1 = first_argmax(el)
    v2, i2 = first_argmax(jnp.where(lane == i1, ninf, el))
    t = jnp.exp(v2 - v1)
    w1 = p_g / (1.0 + t)
    w2 = p_g * t / (1.0 + t)
    ids = jnp.where(lane == 0, i1 - N_GROUPS, jnp.where(lane == 1, i2 - N_GROUPS, 0.0))
    id_ref[...] = ids.astype(jnp.int32)
    wt_ref[...] = jnp.where(lane == 0, w1, jnp.where(lane == 1, w2, 0.0))

    @pl.when(pl.program_id(0) == 0)
    def _():
        cnt_ref[...] = jnp.zeros_like(cnt_ref)

    chosen = jnp.where((lane == i1) | (lane == i2), 1.0, 0.0)
    cnt_ref[...] += jnp.sum(chosen, axis=0, keepdims=True)


def _router(x, g, w_rg, b_rg, w_re, b_re, tm):
    T, D = x.shape
    n_logits = N_GROUPS + N_EXPERTS
    wr = jnp.concatenate([w_rg, jnp.transpose(w_re, (1, 0, 2)).reshape(D, N_EXPERTS)], axis=1)
    wr = jnp.pad(wr.astype(F32), ((0, 0), (0, LANES - n_logits)))
    br = jnp.pad(jnp.concatenate([b_rg, b_re.reshape(-1)]).astype(F32), (0, LANES - n_logits))
    eid, wts, cnt = pl.pallas_call(
        _router_kernel,
        grid=(T // tm,),
        in_specs=[pl.BlockSpec((tm, D), lambda i: (i, 0)),
                  pl.BlockSpec((1, D), lambda i: (0, 0)),
                  pl.BlockSpec((D, LANES), lambda i: (0, 0)),
                  pl.BlockSpec((1, LANES), lambda i: (0, 0))],
        out_specs=[pl.BlockSpec((tm, LANES), lambda i: (i, 0))] * 2
        + [pl.BlockSpec((1, LANES), lambda i: (0, 0))],
        out_shape=[jax.ShapeDtypeStruct((T, LANES), jnp.int32),
                   jax.ShapeDtypeStruct((T, LANES), F32),
                   jax.ShapeDtypeStruct((1, LANES), F32)],
        compiler_params=_params(("arbitrary",)),
        name="router",
    )(x, g.reshape(1, D).astype(F32), wr, br.reshape(1, LANES))
    counts = cnt[0, N_GROUPS:N_GROUPS + N_EXPERTS].astype(jnp.int32)
    return eid[:, :2], wts, counts


def _dispatch_plan(eid, counts, tm, n_tiles):
    T = eid.shape[0]
    n_assign = 2 * T
    order = jnp.argsort(eid.T.reshape(-1), stable=True).astype(jnp.int32)
    tiles_per = (counts + tm - 1) // tm
    tile_end = jnp.cumsum(tiles_per)
    tile_idx = jnp.arange(n_tiles, dtype=jnp.int32)
    tile_expert = jnp.minimum(
        jnp.sum((tile_end[None, :] <= tile_idx[:, None]).astype(jnp.int32), axis=1), N_EXPERTS - 1)
    onehot = (tile_expert[:, None] == jnp.arange(N_EXPERTS, dtype=jnp.int32)[None, :])

    def of_tile(per_expert):
        return jnp.sum(jnp.where(onehot, per_expert[None, :], 0), axis=1)

    local = (tile_idx - of_tile(tile_end - tiles_per)) * tm
    tile_nvalid = jnp.where(tile_idx < tile_end[-1], jnp.clip(of_tile(counts) - local, 0, tm), 0)

    seg_start = jnp.cumsum(counts) - counts
    row_start = (tile_end - tiles_per) * tm
    order_ext = jnp.concatenate([order, jnp.zeros((n_assign,), jnp.int32)])

    def place(e, rows):
        seg = lax.dynamic_slice(order_ext, (seg_start[e],), (n_assign,))
        return lax.dynamic_update_slice(rows, seg, (row_start[e],))

    rows = lax.fori_loop(0, N_EXPERTS, place, jnp.zeros((n_tiles * tm + n_assign,), jnp.int32))
    ids = rows[:n_tiles * tm].reshape(n_tiles, tm)
    row_src = jnp.where(ids >= T, ids - T, ids)
    return tile_expert, tile_nvalid.astype(jnp.int32), row_src, ids


DMA_ISSUE_UNROLL = 8


def _for_rows(n, fn):
    n_groups = n // DMA_ISSUE_UNROLL

    def group(gi, _):
        for u in range(DMA_ISSUE_UNROLL):
            fn(gi * DMA_ISSUE_UNROLL + u, u % 2)
        return 0

    def single(r, _):
        fn(r, 0)
        return 0

    lax.fori_loop(0, n_groups, group, 0)
    lax.fori_loop(n_groups * DMA_ISSUE_UNROLL, n, single, 0)


def _wait_rows(n, max_rows, copy_of_rows):
    k = max_rows
    while k >= 1:
        @pl.when((n & k) != 0)
        def _(k=k):
            copy_of_rows(k).wait()
        k //= 2


def _ffn_kernel(texp_ref, nval_ref, src_ref, src_next_ref, dst_ref, g_ref, wg_ref, wu_ref, wd_ref,
                x_hbm, o_hbm, xbuf, hbuf, obuf, wgu_bf, wd_bf, gsem, ssem):
    i = pl.program_id(0)
    n_tiles = pl.num_programs(0)
    tm = xbuf.shape[1]
    f = wd_ref.shape[0]
    slot = i % 2
    other = 1 - slot
    nval = nval_ref[i]
    nval_next = jnp.where(i + 1 < n_tiles, nval_ref[jnp.minimum(i + 1, n_tiles - 1)], 0)

    def start_gather(idx_ref, s, n):
        _for_rows(n, lambda r, lane: pltpu.make_async_copy(
            x_hbm.at[pl.ds(idx_ref[0, 0, r], 1)], xbuf.at[s, pl.ds(r, 1)], gsem.at[s]).start())

    def wait_gather(s, n):
        _wait_rows(n, tm, lambda k: pltpu.make_async_copy(
            x_hbm.at[pl.ds(0, k)], xbuf.at[s, pl.ds(0, k)], gsem.at[s]))

    def wait_scatter(s, n):
        _wait_rows(n, tm, lambda k: pltpu.make_async_copy(
            obuf.at[s, pl.ds(0, k)], o_hbm.at[pl.ds(0, k)], ssem.at[s]))

    @pl.when(i == 0)
    def _():
        xbuf[...] = jnp.zeros_like(xbuf)
        start_gather(src_ref, 0, nval)

    @pl.when(nval > 0)
    def _():
        start_gather(src_next_ref, other, nval_next)
        wait_gather(slot, nval)

        @pl.when(i >= 2)
        def _():
            wait_scatter(slot, nval_ref[jnp.maximum(i - 2, 0)])

        def norm(r, _):
            rows = pl.ds(pl.multiple_of(r * NORM_ROWS, NORM_ROWS), NORM_ROWS)
            hbuf[rows, :] = _norm_rows(xbuf[slot, rows, :], g_ref[...]).astype(BF16)
            return 0

        lax.fori_loop(0, tm // NORM_ROWS, norm, 0)

        @pl.when((i == 0) | (texp_ref[i] != texp_ref[jnp.maximum(i - 1, 0)]))
        def _():
            wgu_bf[:, :f] = wg_ref[...].astype(BF16)
            wgu_bf[:, f:] = wu_ref[...].astype(BF16)
            wd_bf[...] = wd_ref[...].astype(BF16)

        h = jnp.dot(hbuf[...], wgu_bf[...], preferred_element_type=F32)
        hg, hu = h[:, :f], h[:, f:]
        hid = (hg * (1.0 / (1.0 + jnp.exp(-hg)))) * hu
        obuf[slot] = jnp.dot(hid.astype(BF16), wd_bf[...], preferred_element_type=F32)
        _for_rows(nval, lambda r, lane: pltpu.make_async_copy(
            obuf.at[slot, pl.ds(r, 1)], o_hbm.at[pl.ds(dst_ref[0, 0, r], 1)],
            ssem.at[slot]).start(priority=lane))

        @pl.when(nval_next == 0)
        def _():
            @pl.when(i >= 1)
            def _():
                wait_scatter(other, nval_ref[jnp.maximum(i - 1, 0)])

            wait_scatter(slot, nval)


def _moe_ffn(x, g, plan, wg, wu, wd, layer, tm, n_tiles):
    T, D = x.shape
    tile_expert, tile_nvalid, row_src, row_dst = plan
    f = wg.shape[3]
    row_src = row_src.reshape(n_tiles, 1, tm)

    def idx_spec(step):
        return pl.BlockSpec((1, 1, tm),
                            lambda i, te, nv: (jnp.minimum(i + step, n_tiles - 1), 0, 0),
                            memory_space=pltpu.SMEM)

    grid_spec = pltpu.PrefetchScalarGridSpec(
        num_scalar_prefetch=2,
        grid=(n_tiles,),
        in_specs=[
            idx_spec(0), idx_spec(1), idx_spec(0),
            pl.BlockSpec((1, D), lambda i, te, nv: (0, 0)),
            pl.BlockSpec((None, None, D, f), lambda i, te, nv: (layer, te[i], 0, 0)),
            pl.BlockSpec((None, None, D, f), lambda i, te, nv: (layer, te[i], 0, 0)),
            pl.BlockSpec((None, None, f, D), lambda i, te, nv: (layer, te[i], 0, 0)),
            pl.BlockSpec(memory_space=pl.ANY),
        ],
        out_specs=pl.BlockSpec(memory_space=pl.ANY),
        scratch_shapes=[pltpu.VMEM((2, tm, D), F32), pltpu.VMEM((tm, D), BF16),
                        pltpu.VMEM((2, tm, D), F32),
                        pltpu.VMEM((D, 2 * f), BF16), pltpu.VMEM((f, D), BF16),
                        pltpu.SemaphoreType.DMA((2,)), pltpu.SemaphoreType.DMA((2,))],
    )
    return pl.pallas_call(
        _ffn_kernel,
        grid_spec=grid_spec,
        out_shape=jax.ShapeDtypeStruct((2 * T, D), F32),
        compiler_params=_params(("arbitrary",)),
        name="moe_ffn",
    )(tile_expert, tile_nvalid, row_src, row_src, row_dst.reshape(n_tiles, 1, tm),
      g.reshape(1, D).astype(F32), wg, wu, wd, x)


def _combine_kernel(x_ref, o0_ref, o1_ref, wt_ref, g_ref, y_ref, *, final_norm):
    wt = wt_ref[...]
    y = x_ref[...] + (wt[:, 0:1] * o0_ref[...] + wt[:, 1:2] * o1_ref[...])
    if final_norm:
        y = _norm_rows(y, g_ref[...])
    y_ref[...] = y


def _combine(x, o, wts, g, final_norm, tm):
    T, D = x.shape
    nt = T // tm
    return pl.pallas_call(
        functools.partial(_combine_kernel, final_norm=final_norm),
        grid=(nt,),
        in_specs=[pl.BlockSpec((tm, D), lambda i: (i, 0)),
                  pl.BlockSpec((tm, D), lambda i: (i, 0)),
                  pl.BlockSpec((tm, D), lambda i: (i + nt, 0)),
                  pl.BlockSpec((tm, LANES), lambda i: (i, 0)),
                  pl.BlockSpec((1, D), lambda i: (0, 0))],
        out_specs=pl.BlockSpec((tm, D), lambda i: (i, 0)),
        out_shape=jax.ShapeDtypeStruct((T, D), F32),
        compiler_params=_params(("arbitrary",)),
        name="combine",
    )(x, o, o, wts, g.reshape(1, D).astype(F32))


PROJ_TM, PROJ_TN = 512, 1024
OUT_TM, OUT_TN = 512, 1024
ROUTER_TM = 256
FFN_TM = 256
COMBINE_TM = 256


def kernel(x, norm_attn, w_in, rel_bias, out_norm_dil, out_norm_sb, w_out, norm_ffn,
           w_route_group, b_route_group, w_route_expert, b_route_expert,
           w_gate, w_up, w_down, norm_final):
    B, S, D = x.shape
    T = B * S
    depth = w_in.shape[0]
    n_tiles = 2 * T // FFN_TM + N_EXPERTS
    bias_tab = _dilated_bias_table(rel_bias)
    xt = x.reshape(T, D)
    hd, hs = N_HEADS_DIL, N_HEADS_SB
    w_in_bf = w_in.astype(BF16)
    w_out_bf = w_out.astype(BF16)

    for l in range(depth):
        proj = _norm_matmul([xt], [norm_attn[l]], w_in_bf, l, None, BF16,
                            PROJ_TM, PROJ_TN).reshape(B, S, -1)
        y_a = _dilated_attention(proj, bias_tab, 0, hd, 2 * hd, hd)
        y_b = _sb_attention(proj, 3 * hd, 3 * hd + hs, 3 * hd + 2 * hs, hs)
        xt = _norm_matmul([y_a.reshape(T, -1), y_b.reshape(T, -1)],
                          [out_norm_dil[l], out_norm_sb[l]], w_out_bf, l, xt, F32,
                          OUT_TM, OUT_TN)
        eid, wts, counts = _router(xt, norm_ffn[l], w_route_group[l], b_route_group[l],
                                   w_route_expert[l], b_route_expert[l], ROUTER_TM)
        plan = _dispatch_plan(eid, counts, FFN_TM, n_tiles)
        o = _moe_ffn(xt, norm_ffn[l], plan, w_gate, w_up, w_down, l, FFN_TM, n_tiles)
        xt = _combine(xt, o, wts, norm_final, l == depth - 1, COMBINE_TM)
    return xt.reshape(B, S, D)
```
